```python
import jax, jax.numpy as jnp
from jax import lax
import numpy as np

D_MODEL = 1024
BATCH = 2
SEQ = 8192
DEPTH = 1
DEC_BATCH = 128
DEC_SEQ = 1
PAST_LEN = 8192
PAGE_SIZE = 128

HEAD_DIM = 64
D_MIX = D_MODEL
D_RWKV = D_MIX // 2
D_FOX = D_MIX - D_RWKV
H_RWKV = D_RWKV // HEAD_DIM
H_FOX = D_FOX // HEAD_DIM
W_LORA = 32
A_LORA = 32
G_LORA = 96
D_FF = 2816
CONV_W = 3
Q_BLOCK = 128
RMS_EPS = 1e-6
GN_EPS = 64e-5
ATTN_SCALE = HEAD_DIM ** -0.5
F32 = jnp.float32
RWKV_SPLITS = [D_RWKV, 2 * D_RWKV, 3 * D_RWKV, 3 * D_RWKV + W_LORA, 3 * D_RWKV + W_LORA + A_LORA]
D_RWKV_IN = 3 * D_RWKV + W_LORA + A_LORA + G_LORA
D_FOX_IN = 3 * D_FOX + H_FOX
D_IN = D_RWKV_IN + D_FOX_IN

kernel_name = 'rwkv7_fox_hybrid_decode_step'


def rms_norm(x, g):
    xf = x.astype(F32)
    y = xf * lax.rsqrt(jnp.mean(xf * xf, axis=-1, keepdims=True) + RMS_EPS)
    return (y * g.astype(F32)).astype(x.dtype)


def rwkv_time_mix(z, shift_prev, s0, mu, w0, w_up, a0, a_up, g_up, k_k, k_a, r_k, ln_w, ln_b):
    B, T, _ = z.shape
    z_full = jnp.concatenate([shift_prev[:, None, :].astype(z.dtype), z], axis=1)
    zm = z + mu * (z_full[:, :-1] - z)
    r, k, v, wd, ad, gd = jnp.split(zm, RWKV_SPLITS, axis=-1)
    w = (w0 + jnp.tanh(wd) @ w_up).astype(F32)
    decay = jnp.exp(-jnp.exp(-jax.nn.softplus(-w) - 0.5))
    a = jax.nn.sigmoid((a0 + ad @ a_up).astype(F32))
    g = jax.nn.sigmoid(gd) @ g_up
    kk = (k * k_k).astype(F32)
    k = k.astype(F32) * (1 + (a - 1) * k_a)
    heads = lambda t: t.astype(F32).reshape(B, T, H_RWKV, HEAD_DIM)
    r, k, v, decay, a, kk = map(heads, (r, k, v, decay, a, kk))
    kk = kk / jnp.maximum(jnp.linalg.norm(kk, axis=-1, keepdims=True), 1e-12)

    def step(S, inp):
        r_t, k_t, v_t, d_t, kk_t, a_t = inp
        sa = jnp.einsum('bhij,bhj->bhi', S, -kk_t)
        S = (S * d_t[:, :, None, :] + sa[..., None] * (kk_t * a_t)[:, :, None, :]
             + v_t[..., None] * k_t[:, :, None, :])
        return S, jnp.einsum('bhij,bhj->bhi', S, r_t)

    xs = tuple(jnp.moveaxis(t, 1, 0) for t in (r, k, v, decay, kk, a))
    s_final, ys = lax.scan(step, s0.astype(F32), xs)
    ys = jnp.moveaxis(ys, 0, 1)
    mean = jnp.mean(ys, axis=-1, keepdims=True)
    var = jnp.mean(jnp.square(ys - mean), axis=-1, keepdims=True)
    yn = ((ys - mean) * lax.rsqrt(var + GN_EPS)).reshape(B, T, D_RWKV) * ln_w + ln_b
    bonus = (jnp.sum(r * k * r_k, axis=-1, keepdims=True) * v).reshape(B, T, D_RWKV)
    out = (yn + bonus) * g.astype(F32)
    return out.astype(z.dtype), s_final.astype(s0.dtype), z_full[:, -1]


def fox_prompt(q, k, v, log_f):
    B, T, H, _ = q.shape
    n_blk = T // Q_BLOCK
    cum = jnp.cumsum(log_f, axis=1).transpose(0, 2, 1)
    kf, vf = k.astype(F32), v.astype(F32)
    q_blk = q.astype(F32).reshape(B, n_blk, Q_BLOCK, H, HEAD_DIM).transpose(1, 0, 2, 3, 4) * ATTN_SCALE
    cq_blk = cum.reshape(B, H, n_blk, Q_BLOCK).transpose(2, 0, 1, 3)
    key_pos = jnp.arange(T)

    def block(args):
        qb, cqb, i = args
        s = jnp.einsum('bqhd,bkhd->bhqk', qb, kf) + cqb[..., None] - cum[:, :, None, :]
        q_pos = i * Q_BLOCK + jnp.arange(Q_BLOCK)
        s = jnp.where(key_pos[None, :] <= q_pos[:, None], s, -jnp.inf)
        p = jax.nn.softmax(s, axis=-1)
        return jnp.einsum('bhqk,bkhd->bqhd', p, vf)

    o = lax.map(block, (q_blk, cq_blk, jnp.arange(n_blk)))
    return o.transpose(1, 0, 2, 3, 4).reshape(B, T, H * HEAD_DIM).astype(q.dtype)


def online_update(carry, s, vb):
    m, l, acc = carry
    m_new = jnp.maximum(m, jnp.max(s, axis=-1))
    corr = jnp.exp(m - m_new)
    p = jnp.exp(s - m_new[..., None])
    l = l * corr + jnp.sum(p, axis=-1)
    acc = acc * corr[..., None] + jnp.einsum('bhqk,bkhd->bhqd', p, vb)
    return (m_new, l, acc)


def fox_sample(q, k, v, log_f, cache_k, cache_v, cache_logf, page_table, layer):
    DB, T, H, _ = q.shape
    n_pages = page_table.shape[1]
    past_len = n_pages * PAGE_SIZE
    past_logf = cache_logf[layer, page_table].reshape(DB, past_len, H).astype(F32)
    cum = jnp.cumsum(jnp.concatenate([past_logf, log_f], axis=1), axis=1).transpose(0, 2, 1)
    c_past, c_new = cum[:, :, :past_len], cum[:, :, past_len:]
    qf = q.astype(F32) * ATTN_SCALE

    def page_step(carry, inp):
        pt, cp = inp
        kp = cache_k[layer, pt].astype(F32)
        vp = cache_v[layer, pt].astype(F32)
        s = jnp.einsum('bqhd,bkhd->bhqk', qf, kp) + c_new[..., None] - cp[:, :, None, :]
        return online_update(carry, s, vp), None

    init = (jnp.full((DB, H, T), -jnp.inf, F32), jnp.zeros((DB, H, T), F32),
            jnp.zeros((DB, H, T, HEAD_DIM), F32))
    xs = (page_table.T, c_past.reshape(DB, H, n_pages, PAGE_SIZE).transpose(2, 0, 1, 3))
    carry, _ = lax.scan(page_step, init, xs)
    s_new = jnp.einsum('bqhd,bkhd->bhqk', qf, k.astype(F32)) + c_new[..., None] - c_new[:, :, None, :]
    s_new = jnp.where(jnp.tril(jnp.ones((T, T), bool)), s_new, -jnp.inf)
    _, l, acc = online_update(carry, s_new, v.astype(F32))
    o = acc / l[..., None]
    return o.transpose(0, 2, 1, 3).reshape(DB, T, H * HEAD_DIM).astype(q.dtype)


def conv_ffn(h, conv_prev, w_up, conv_w, conv_b, w_down):
    T = h.shape[1]
    u = h @ w_up
    u_full = jnp.concatenate([conv_prev.astype(u.dtype), u], axis=1)
    uc = conv_b + sum(u_full[:, j:j + T] * conv_w[j] for j in range(CONV_W))
    gate, val = jnp.split(uc, 2, axis=-1)
    return (jax.nn.silu(gate) * val) @ w_down, u_full[:, -(CONV_W - 1):]


def setup_inputs(seed: int = 0) -> dict:
    key = jax.random.key(seed)
    k = jax.random.split(key, 34)
    nrm = lambda kk, shape, scale=1.0: scale * jax.random.normal(kk, shape, F32)
    unif = lambda kk, shape, lo, hi: jax.random.uniform(kk, shape, F32, lo, hi)
    n_pages = PAST_LEN // PAGE_SIZE
    n_phys = (DEC_BATCH * n_pages * 5) // 4
    page_table = jax.random.permutation(k[7], n_phys)[:DEC_BATCH * n_pages].reshape(DEC_BATCH, n_pages).astype(jnp.int32)
    return {
        'x_prompt': nrm(k[0], (BATCH, SEQ, D_MODEL)),
        'x_sample': nrm(k[1], (DEC_BATCH, DEC_SEQ, D_MODEL)),
        'c_prompt': nrm(k[2], (BATCH, D_MODEL)),
        'c_sample': nrm(k[3], (DEC_BATCH, D_MODEL)),
        'cache_k': nrm(k[4], (DEPTH, n_phys, PAGE_SIZE, H_FOX, HEAD_DIM)),
        'cache_v': nrm(k[5], (DEPTH, n_phys, PAGE_SIZE, H_FOX, HEAD_DIM)),
        'cache_logf': jax.nn.log_sigmoid(3.0 + nrm(k[6], (DEPTH, n_phys, PAGE_SIZE, H_FOX))),
        'page_table': page_table,
        'state_wkv': nrm(k[8], (DEPTH, DEC_BATCH, H_RWKV, HEAD_DIM, HEAD_DIM), 0.5),
        'state_shift': nrm(k[9], (DEPTH, DEC_BATCH, D_RWKV_IN)),
        'state_ffn_conv': nrm(k[10], (DEPTH, DEC_BATCH, CONV_W - 1, 2 * D_FF)),
        'w_ada': nrm(k[11], (DEPTH, D_MODEL, 6 * D_MODEL), 0.5 * D_MODEL ** -0.5),
        'b_ada': nrm(k[12], (DEPTH, 6 * D_MODEL), 0.01),
        'g_attn_norm': 1.0 + nrm(k[13], (DEPTH, D_MODEL), 0.01),
        'w_in': nrm(k[14], (DEPTH, D_MODEL, D_IN), D_MODEL ** -0.5),
        'b_forget': unif(k[15], (DEPTH, H_FOX), 1.0, 5.0),
        'rwkv_mu': unif(k[16], (DEPTH, D_RWKV_IN), 0.0, 1.0),
        'rwkv_w0': unif(k[17], (DEPTH, D_RWKV), -6.0, -0.5),
        'rwkv_w_up': nrm(k[18], (DEPTH, W_LORA, D_RWKV), 0.5 * W_LORA ** -0.5),
        'rwkv_a0': nrm(k[19], (DEPTH, D_RWKV), 0.5),
        'rwkv_a_up': nrm(k[20], (DEPTH, A_LORA, D_RWKV), A_LORA ** -0.5),
        'rwkv_g_up': nrm(k[21], (DEPTH, G_LORA, D_RWKV), G_LORA ** -0.5),
        'rwkv_k_k': 0.85 + nrm(k[22], (DEPTH, D_RWKV), 0.05),
        'rwkv_k_a': 1.0 + nrm(k[23], (DEPTH, D_RWKV), 0.05),
        'rwkv_r_k': nrm(k[24], (DEPTH, H_RWKV, HEAD_DIM), 0.1),
        'rwkv_ln_w': 1.0 + nrm(k[25], (DEPTH, D_RWKV), 0.01),
        'rwkv_ln_b': nrm(k[26], (DEPTH, D_RWKV), 0.01),
        'w_out': nrm(k[27], (DEPTH, D_MIX, D_MODEL), D_MIX ** -0.5),
        'g_ffn_norm': 1.0 + nrm(k[28], (DEPTH, D_MODEL), 0.01),
        'w_ffn_up': nrm(k[29], (DEPTH, D_MODEL, 2 * D_FF), D_MODEL ** -0.5),
        'ffn_conv_w': nrm(k[30], (DEPTH, CONV_W, 2 * D_FF), 0.5),
        'ffn_conv_b': nrm(k[31], (DEPTH, 2 * D_FF), 0.01),
        'w_ffn_down': nrm(k[32], (DEPTH, D_FF, D_MODEL), D_FF ** -0.5),
        'g_final_norm': 1.0 + nrm(k[33], (D_MODEL,), 0.01),
    }


def reference(x_prompt, x_sample, c_prompt, c_sample, cache_k, cache_v, cache_logf, page_table,
              state_wkv, state_shift, state_ffn_conv, w_ada, b_ada, g_attn_norm, w_in, b_forget,
              rwkv_mu, rwkv_w0, rwkv_w_up, rwkv_a0, rwkv_a_up, rwkv_g_up, rwkv_k_k, rwkv_k_a, rwkv_r_k,
              rwkv_ln_w, rwkv_ln_b, w_out, g_ffn_norm, w_ffn_up, ffn_conv_w, ffn_conv_b, w_ffn_down,
              g_final_norm):

    def run_group(x, c, init_state, fox_attend):
        B, T = x.shape[0], x.shape[1]
        nk, nv, nlf, nwkv, nshift, nconv = [], [], [], [], [], []
        for l in range(DEPTH):
            wkv0, shift0, conv0 = init_state(l)
            mod = jax.nn.silu(c) @ w_ada[l] + b_ada[l]
            sh_a, sc_a, gt_a, sh_f, sc_f, gt_f = [m[:, None, :] for m in jnp.split(mod, 6, axis=-1)]
            h = rms_norm(x, g_attn_norm[l]) * (1 + sc_a) + sh_a
            z = h @ w_in[l]
            o_r, wkv1, shift1 = rwkv_time_mix(
                z[..., :D_RWKV_IN], shift0, wkv0, rwkv_mu[l], rwkv_w0[l], rwkv_w_up[l], rwkv_a0[l],
                rwkv_a_up[l], rwkv_g_up[l], rwkv_k_k[l], rwkv_k_a[l], rwkv_r_k[l], rwkv_ln_w[l], rwkv_ln_b[l])
            q, kx, vx, f = jnp.split(z[..., D_RWKV_IN:], [D_FOX, 2 * D_FOX, 3 * D_FOX], axis=-1)
            q, kx, vx = (t.reshape(B, T, H_FOX, HEAD_DIM) for t in (q, kx, vx))
            log_f = jax.nn.log_sigmoid((f + b_forget[l]).astype(F32))
            o_f = fox_attend(l, q, kx, vx, log_f)
            x = x + gt_a * (jnp.concatenate([o_r, o_f], axis=-1) @ w_out[l])
            h = rms_norm(x, g_ffn_norm[l]) * (1 + sc_f) + sh_f
            o_c, conv1 = conv_ffn(h, conv0, w_ffn_up[l], ffn_conv_w[l], ffn_conv_b[l], w_ffn_down[l])
            x = x + gt_f * o_c
            nk.append(kx); nv.append(vx); nlf.append(log_f.astype(x.dtype))
            nwkv.append(wkv1); nshift.append(shift1); nconv.append(conv1)
        y = rms_norm(x, g_final_norm)
        return y, (jnp.stack(nk), jnp.stack(nv), jnp.stack(nlf), jnp.stack(nwkv),
                   jnp.stack(nshift), jnp.stack(nconv))

    bp = x_prompt.shape[0]
    prompt_init = lambda l: (jnp.zeros((bp, H_RWKV, HEAD_DIM, HEAD_DIM), F32),
                             jnp.zeros((bp, D_RWKV_IN), x_prompt.dtype),
                             jnp.zeros((bp, CONV_W - 1, 2 * D_FF), x_prompt.dtype))
    sample_init = lambda l: (state_wkv[l], state_shift[l], state_ffn_conv[l])
    prompt_attend = lambda l, q, kx, vx, lf: fox_prompt(q, kx, vx, lf)
    sample_attend = lambda l, q, kx, vx, lf: fox_sample(q, kx, vx, lf, cache_k, cache_v, cache_logf, page_table, l)

    y_prompt, (k_prompt, v_prompt, logf_prompt, wkv_prompt, shift_prompt, conv_prompt) = run_group(
        x_prompt, c_prompt, prompt_init, prompt_attend)
    y_sample, (k_sample, v_sample, logf_sample, wkv_sample, shift_sample, conv_sample) = run_group(
        x_sample, c_sample, sample_init, sample_attend)
    return (y_prompt, y_sample, k_prompt, v_prompt, logf_prompt, wkv_prompt, shift_prompt, conv_prompt,
            k_sample, v_sample, logf_sample, wkv_sample, shift_sample, conv_sample)
```

```python
import functools

import jax
import jax.numpy as jnp
from jax import lax
from jax.experimental import pallas as pl
from jax.experimental.pallas import tpu as pltpu

F32 = jnp.float32
BF16 = jnp.bfloat16

D_MODEL = 1024
HEAD_DIM = 64
N_HEADS = 8
D_GROUP = N_HEADS * HEAD_DIM
W_LORA, A_LORA, G_LORA = 32, 32, 96
D_RWKV_IN = 3 * D_GROUP + W_LORA + A_LORA + G_LORA
ZR_PAD = 1792
LORA_OFF = 3 * D_GROUP
LORA_W = ZR_PAD - LORA_OFF
F_PAD = 128
W_IN_PAD = ZR_PAD + 3 * D_GROUP + F_PAD
D_FF = 2816
CONV_W = 3
PAGE = 128
RMS_EPS = 1e-6
GN_EPS = 64e-5
ATTN_SCALE = HEAD_DIM ** -0.5
LANES = 128
VMEM_LIMIT = 56 * 1024 * 1024

ROW_TILE = 256
RWKV_TILE = 256
CHUNK = 64
ATTN_TILE = 512
FF_CHUNK = 256
PAGES_PER_STEP = 8


def _params(**kw):
    return pltpu.CompilerParams(vmem_limit_bytes=VMEM_LIMIT, **kw)


def _dot(a, b):
    return jnp.dot(a, b, preferred_element_type=F32)


def _dot_nt(a, b):
    return lax.dot_general(a, b, (((1,), (1,)), ((), ())), preferred_element_type=F32)


def _dot_tn(a, b):
    return lax.dot_general(a, b, (((0,), (0,)), ((), ())), preferred_element_type=F32)


def _split3(x):
    hi = x.astype(BF16)
    r1 = x - hi.astype(F32)
    mid = r1.astype(BF16)
    lo = (r1 - mid.astype(F32)).astype(BF16)
    return hi, mid, lo


def _rms(x, g):
    return x * lax.rsqrt(jnp.mean(x * x, axis=-1, keepdims=True) + RMS_EPS) * g


def _mod_kernel(c_ref, w_ref, b_ref, o_ref):
    c = c_ref[...]
    s = (c * jax.nn.sigmoid(c)).astype(BF16)
    o_ref[...] = _dot(s, w_ref[...]) + b_ref[...]


def _mod(c, w_bf, b):
    m, n, tn = c.shape[0], w_bf.shape[1], 1536
    return pl.pallas_call(
        _mod_kernel, grid=(n // tn,),
        in_specs=[pl.BlockSpec((m, D_MODEL), lambda j: (0, 0)),
                  pl.BlockSpec((D_MODEL, tn), lambda j: (0, j)),
                  pl.BlockSpec((1, tn), lambda j: (0, j))],
        out_specs=pl.BlockSpec((m, tn), lambda j: (0, j)),
        out_shape=jax.ShapeDtypeStruct((m, n), F32),
        compiler_params=_params(), name="adaln_mod")(c, w_bf, b)


def _inproj_kernel(per_row, x_ref, sh_ref, sc_ref, g_ref, w_ref, bf_ref,
                   zr_ref, qb_ref, k_ref, v_ref, kb_ref, vb_ref, lf_ref, lft_ref):
    sh = sh_ref[...] if per_row else sh_ref[0]
    sc = sc_ref[...] if per_row else sc_ref[0]
    h = (_rms(x_ref[...], g_ref[...]) * (1 + sc) + sh).astype(BF16)
    zr_ref[...] = _dot(h, w_ref[:, 0:ZR_PAD])
    o = ZR_PAD
    qb_ref[...] = (_dot(h, w_ref[:, o:o + D_GROUP]) * ATTN_SCALE).astype(BF16)
    k = _dot(h, w_ref[:, o + D_GROUP:o + 2 * D_GROUP])
    k_ref[...] = k
    kb_ref[...] = k.astype(BF16)
    v = _dot(h, w_ref[:, o + 2 * D_GROUP:o + 3 * D_GROUP])
    v_ref[...] = v
    vb_ref[...] = v.astype(BF16)
    f = _dot(h, w_ref[:, o + 3 * D_GROUP:o + 3 * D_GROUP + F_PAD])
    lf = jax.nn.log_sigmoid(f + bf_ref[...])
    lf_ref[...] = lf[:, 0:N_HEADS]
    lft_ref[...] = lf.T[0:N_HEADS, :]


def _inproj(x2d, mod, per_row, tiles_per_b, g, w_bf, bfp, tm):
    rows = x2d.shape[0]
    if per_row:
        sh_spec = pl.BlockSpec((tm, D_MODEL), lambda i: (i, 0))
        sc_spec = pl.BlockSpec((tm, D_MODEL), lambda i: (i, 1))
    else:
        sh_spec = pl.BlockSpec((1, 1, D_MODEL), lambda i: (i // tiles_per_b, 0, 0))
        sc_spec = pl.BlockSpec((1, 1, D_MODEL), lambda i: (i // tiles_per_b, 0, 1))
    row_spec = lambda n: pl.BlockSpec((tm, n), lambda i: (i, 0))
    const = lambda shape: pl.BlockSpec(shape, lambda i: (0, 0))
    return pl.pallas_call(
        functools.partial(_inproj_kernel, per_row), grid=(rows // tm,),
        in_specs=[row_spec(D_MODEL), sh_spec, sc_spec, const((1, D_MODEL)),
                  const((D_MODEL, W_IN_PAD)), const((1, F_PAD))],
        out_specs=[row_spec(ZR_PAD), row_spec(D_GROUP), row_spec(D_GROUP), row_spec(D_GROUP),
                   row_spec(D_GROUP), row_spec(D_GROUP), row_spec(N_HEADS),
                   pl.BlockSpec((N_HEADS, tm), lambda i: (0, i))],
        out_shape=[jax.ShapeDtypeStruct((rows, ZR_PAD), F32),
                   jax.ShapeDtypeStruct((rows, D_GROUP), BF16),
                   jax.ShapeDtypeStruct((rows, D_GROUP), F32),
                   jax.ShapeDtypeStruct((rows, D_GROUP), F32),
                   jax.ShapeDtypeStruct((rows, D_GROUP), BF16),
                   jax.ShapeDtypeStruct((rows, D_GROUP), BF16),
                   jax.ShapeDtypeStruct((rows, N_HEADS), F32),
                   jax.ShapeDtypeStruct((N_HEADS, rows), F32)],
        compiler_params=_params(dimension_semantics=("parallel",)), name="in_proj")(
            x2d, mod, mod, g, w_bf, bfp)


def _cumsum_kernel(nb, t_len, x_ref, o_ref):
    blk = 256
    r = lax.broadcasted_iota(jnp.int32, (blk, blk), 0)
    c = lax.broadcasted_iota(jnp.int32, (blk, blk), 1)
    upper = (r <= c).astype(BF16)
    for b in range(nb):
        carry = jnp.zeros((N_HEADS, 1), F32)
        for j in range(t_len // blk):
            sl = slice(b * t_len + j * blk, b * t_len + (j + 1) * blk)
            hi, mid, lo = _split3(x_ref[:, sl])
            cs = (_dot(lo, upper) + _dot(mid, upper)) + _dot(hi, upper) + carry
            o_ref[:, sl] = cs
            carry = cs[:, blk - 1:blk]


def _cumsum(lft, nb, t_len):
    return pl.pallas_call(
        functools.partial(_cumsum_kernel, nb, t_len),
        out_shape=jax.ShapeDtypeStruct(lft.shape, F32),
        compiler_params=_params(), name="logf_cumsum")(lft)


def _rwkv_mix(zm, w0, wup, a0, aup, gup, k_k, k_a):
    r = zm[:, 0:D_GROUP]
    k = zm[:, D_GROUP:2 * D_GROUP]
    v = zm[:, 2 * D_GROUP:3 * D_GROUP]
    lora = zm[:, LORA_OFF:ZR_PAD]
    w = w0 + _dot(jnp.tanh(lora).astype(BF16), wup)
    logd = -jnp.exp(-jax.nn.softplus(-w) - 0.5)
    a = jax.nn.sigmoid(a0 + _dot(lora.astype(BF16), aup))
    g = _dot(jax.nn.sigmoid(lora).astype(BF16), gup)
    kk = k * k_k
    k = k * (1 + (a - 1) * k_a)
    return r, k, v, logd, kk, a, g


def _group_norm(y):
    mean = jnp.mean(y, axis=-1, keepdims=True)
    var = jnp.mean(jnp.square(y - mean), axis=-1, keepdims=True)
    return (y - mean) * lax.rsqrt(var + GN_EPS)


def _rwkv_kernel(z_ref, sh0_ref, mu_ref, w0_ref, wup_ref, a0_ref, aup_ref, gup_ref, kk_ref, ka_ref,
                 rk_ref, lnw_ref, lnb_ref, o_ref, st_ref, prev_s, st_s, hm_s, yn_s, bon_s):
    tt, c = RWKV_TILE, CHUNK
    t = pl.program_id(1)

    @pl.when(t == 0)
    def _():
        prev_s[0:1, :] = sh0_ref[0]
        st_s[...] = jnp.zeros_like(st_s)

    z = z_ref[...]
    row = lax.broadcasted_iota(jnp.int32, (tt, 1), 0)
    zprev = jnp.where(row == 0, prev_s[0:1, :], pltpu.roll(z, 1, axis=0))
    prev_s[0:1, :] = z[tt - 1:tt, :]
    zm = z + mu_ref[...] * (zprev - z)
    r, k, v, logd, kk, a, g = _rwkv_mix(zm, w0_ref[...], wup_ref[...], a0_ref[...], aup_ref[...],
                                        gup_ref[...], kk_ref[...], ka_ref[...])
    ri = lax.broadcasted_iota(jnp.int32, (tt, tt), 0)
    ci = lax.broadcasted_iota(jnp.int32, (tt, tt), 1)
    tri = ((ri // c == ci // c) & (ci <= ri)).astype(BF16)
    hi, mid, lo = _split3(logd)
    cl = (_dot(tri, lo) + _dot(tri, mid)) + _dot(tri, hi)
    for h in range(N_HEADS):
        sl = slice(h * HEAD_DIM, (h + 1) * HEAD_DIM)
        for n, val in enumerate((r, k, v, kk, a, logd, cl)):
            hm_s[n, h] = val[:, sl]

    rr = lax.broadcasted_iota(jnp.int32, (c, c), 0)
    cc = lax.broadcasted_iota(jnp.int32, (c, c), 1)
    strict = (cc < rr).astype(F32)
    eye = rr == cc
    r2 = lax.broadcasted_iota(jnp.int32, (c, 2 * c), 0)
    c2 = lax.broadcasted_iota(jnp.int32, (c, 2 * c), 1)
    sgn = jnp.where(c2 < c, jnp.where(c2 <= r2, 1.0, 0.0), jnp.where(c2 - c <= r2, -1.0, 0.0)).astype(F32)
    zeros_cn = jnp.zeros((c, HEAD_DIM), F32)

    def head_body(h, carry):
        r_, k_, v_, kk_, a_, ld_, cl_ = (hm_s[n, h] for n in range(7))
        nrm = jnp.sqrt(jnp.sum(kk_ * kk_, axis=-1, keepdims=True))
        kkn = kk_ / jnp.maximum(nrm, 1e-12)
        b_ = kkn * a_
        g_inc = jnp.exp(cl_)
        g_exc = jnp.exp(cl_ - ld_)
        g_inv = jnp.exp(-cl_)
        st = st_s[h]
        ys = []
        for i in range(tt // c):
            s = slice(i * c, (i + 1) * c)
            cl_end = cl_[(i + 1) * c - 1:(i + 1) * c, :]
            e_end = jnp.exp(cl_end - cl_[s])
            kkg = kkn[s] * g_exc[s]
            rg = r_[s] * g_inc[s]
            kd = (k_[s] * g_inv[s]).astype(BF16)
            bd = (b_[s] * g_inv[s]).astype(BF16)
            kkg_b = kkg.astype(BF16)
            a_k = _dot_nt(kkg_b, kd) * strict
            a_b = _dot_nt(kkg_b, bd) * strict
            ll = _dot_nt(rg.astype(BF16), jnp.concatenate([kd, bd], axis=0)) * sgn
            akv = _dot(a_k.astype(BF16), v_[s].astype(BF16))
            x = jnp.concatenate([akv, kkg], axis=1)
            p = (-a_b).astype(BF16)
            x = x + _dot(p, x.astype(BF16))
            n = 2
            while n < c:
                p = _dot(p, p).astype(BF16)
                x = x + _dot(p, x.astype(BF16))
                n *= 2
            rhs = jnp.concatenate([jnp.concatenate([v_[s], zeros_cn], axis=1), x], axis=0).astype(BF16)
            yq = _dot(ll.astype(BF16), rhs)
            kb = jnp.concatenate([k_[s] * e_end, -(b_[s] * e_end)], axis=0).astype(BF16)
            sw = _dot_tn(kb, rhs)
            st_b = st.astype(BF16)
            qc = rg + yq[:, HEAD_DIM:]
            ys.append(_dot(qc.astype(BF16), st_b) + yq[:, 0:HEAD_DIM])
            m = jnp.where(eye, jnp.broadcast_to(jnp.exp(cl_end), (c, c)), 0.0) + sw[:, HEAD_DIM:]
            st = _dot(m.astype(BF16), st_b) + sw[:, 0:HEAD_DIM]
        st_s[h] = st
        y = jnp.concatenate(ys, axis=0)
        yn_s[h] = _group_norm(y)
        bon_s[h] = jnp.sum(r_ * k_ * rk_ref[h], axis=-1, keepdims=True) * v_
        return carry

    lax.fori_loop(0, N_HEADS, head_body, 0)
    yn = jnp.concatenate([yn_s[h] for h in range(N_HEADS)], axis=1)
    bon = jnp.concatenate([bon_s[h] for h in range(N_HEADS)], axis=1)
    o_ref[...] = ((yn * lnw_ref[...] + lnb_ref[...] + bon) * g).astype(BF16)
    st_ref[...] = st_s[...]


def _rwkv_prompt(zr, shift0, nb, t_len, rp):
    tt = RWKV_TILE
    nt = t_len // tt
    const = lambda shape: pl.BlockSpec(shape, lambda b, t: (0,) * len(shape))
    return pl.pallas_call(
        _rwkv_kernel, grid=(nb, nt),
        in_specs=[pl.BlockSpec((tt, ZR_PAD), lambda b, t: (b * nt + t, 0)),
                  pl.BlockSpec((1, 1, ZR_PAD), lambda b, t: (b, 0, 0)),
                  const((1, ZR_PAD)), const((1, D_GROUP)), const((LORA_W, D_GROUP)),
                  const((1, D_GROUP)), const((LORA_W, D_GROUP)), const((LORA_W, D_GROUP)),
                  const((1, D_GROUP)), const((1, D_GROUP)), const((N_HEADS, 1, HEAD_DIM)),
                  const((1, D_GROUP)), const((1, D_GROUP))],
        out_specs=[pl.BlockSpec((tt, D_GROUP), lambda b, t: (b * nt + t, 0)),
                   pl.BlockSpec((N_HEADS, HEAD_DIM, HEAD_DIM), lambda b, t: (b, 0, 0))],
        out_shape=[jax.ShapeDtypeStruct((nb * t_len, D_GROUP), BF16),
                   jax.ShapeDtypeStruct((nb * N_HEADS, HEAD_DIM, HEAD_DIM), F32)],
        scratch_shapes=[pltpu.VMEM((8, ZR_PAD), F32),
                        pltpu.VMEM((N_HEADS, HEAD_DIM, HEAD_DIM), F32),
                        pltpu.VMEM((7, N_HEADS, tt, HEAD_DIM), F32),
                        pltpu.VMEM((N_HEADS, tt, HEAD_DIM), F32),
                        pltpu.VMEM((N_HEADS, tt, HEAD_DIM), F32)],
        compiler_params=_params(dimension_semantics=("arbitrary", "arbitrary")), name="rwkv_scan")(
            zr, shift0, rp["mu"], rp["w0"], rp["wup"], rp["a0"], rp["aup"], rp["gup"], rp["k_k"],
            rp["k_a"], rp["r_k3"], rp["ln_w"], rp["ln_b"])


def _attn_kernel(q_ref, k_ref, v_ref, c_ref, o_ref):
    tq = ATTN_TILE
    qi = pl.program_id(2)
    q2 = q_ref[...]
    lane = lax.broadcasted_iota(jnp.int32, (1, LANES), 1)
    lo = lane < HEAD_DIM
    zero = jnp.zeros_like(q2)
    q_h = (jnp.where(lo, q2, zero), jnp.where(lo, zero, q2))
    c_ref0 = c_ref[0, :, pl.ds(pl.multiple_of(qi * tq, tq), LANES)][:, 0:1]
    row = lax.broadcasted_iota(jnp.int32, (tq, tq), 0)
    col = lax.broadcasted_iota(jnp.int32, (tq, tq), 1)
    causal = col <= row

    def step(j, carry, masked):
        off = pl.multiple_of(j * tq, tq)
        kb = k_ref[pl.ds(off, tq), :]
        vb = v_ref[pl.ds(off, tq), :]
        ck = c_ref[0, :, pl.ds(off, tq)]
        out = []
        for hd in range(2):
            m, l, acc = carry[3 * hd:3 * hd + 3]
            s = _dot_nt(q_h[hd], kb) + (c_ref0[hd:hd + 1, :] - ck[hd:hd + 1, :])
            if masked:
                s = jnp.where(causal, s, -jnp.inf)
            m_new = jnp.maximum(m, jnp.max(s, axis=-1, keepdims=True))
            alpha = jnp.exp(m - m_new)
            p = jnp.exp(s - m_new)
            l = alpha * l + jnp.sum(p, axis=-1, keepdims=True)
            acc = alpha * acc + _dot(p.astype(BF16), vb)
            out += [m_new, l, acc]
        return tuple(out)

    init = (jnp.full((tq, 1), -jnp.inf, F32), jnp.zeros((tq, 1), F32), jnp.zeros((tq, LANES), F32)) * 2
    carry = lax.fori_loop(0, qi, lambda j, cr: step(j, cr, False), init)
    _, l0, acc0, _, l1, acc1 = step(qi, carry, True)
    o_ref[...] = jnp.where(lo, acc0 / l0, acc1 / l1).astype(BF16)


def _attn_prompt(qb, kb, vb, cum, nb, t_len):
    tq = ATTN_TILE
    nq = t_len // tq
    n_pairs = D_GROUP // LANES
    cum3 = cum.reshape(n_pairs, 2, nb * t_len)
    return pl.pallas_call(
        _attn_kernel, grid=(nb, n_pairs, nq),
        in_specs=[pl.BlockSpec((tq, LANES), lambda b, hp, qi: (b * nq + qi, hp)),
                  pl.BlockSpec((t_len, LANES), lambda b, hp, qi: (b, hp)),
                  pl.BlockSpec((t_len, LANES), lambda b, hp, qi: (b, hp)),
                  pl.BlockSpec((1, 2, t_len), lambda b, hp, qi: (hp, 0, b))],
        out_specs=pl.BlockSpec((tq, LANES), lambda b, hp, qi: (b * nq + qi, hp)),
        out_shape=jax.ShapeDtypeStruct((nb * t_len, D_GROUP), BF16),
        compiler_params=_params(dimension_semantics=("parallel", "parallel", "arbitrary")),
        name="fox_prompt")(qb, kb, vb, cum3)


def _ffn_front(x, o_r, o_f, gta, shf, scf, wout_ref, gffn):
    attn = _dot(o_r, wout_ref[0:D_GROUP, :]) + _dot(o_f, wout_ref[D_GROUP:2 * D_GROUP, :])
    x1 = x + gta * attn
    return x1, (_rms(x1, gffn) * (1 + scf) + shf).astype(BF16)


def _ffn_prompt_kernel(tpb, x_ref, or_ref, of_ref, gta_ref, shf_ref, scf_ref, gtf_ref, wout_ref, gffn_ref,
                       wup_ref, cw_ref, cb_ref, wdn_ref, gfin_ref, cp_ref, y_ref, conv_ref, prev_s):
    i = pl.program_id(0)

    @pl.when(i % tpb == 0)
    def _():
        prev_s[0:2, :] = cp_ref[0]

    x1, h2 = _ffn_front(x_ref[...], or_ref[...], of_ref[...], gta_ref[0], shf_ref[0], scf_ref[0],
                        wout_ref, gffn_ref[...])
    tm = x1.shape[0]
    row = lax.broadcasted_iota(jnp.int32, (tm, 1), 0)

    def conv(cols):
        u = _dot(h2, wup_ref[:, cols])
        p0 = prev_s[0:1, cols]
        p1 = prev_s[1:2, cols]
        u1 = jnp.where(row == 0, p1, pltpu.roll(u, 1, axis=0))
        u2 = jnp.where(row == 0, p0, jnp.where(row == 1, p1, pltpu.roll(u, 2, axis=0)))
        prev_s[0:2, cols] = u[tm - 2:tm, :]
        return cb_ref[:, cols] + cw_ref[0:1, cols] * u2 + cw_ref[1:2, cols] * u1 + cw_ref[2:3, cols] * u

    acc = jnp.zeros((tm, D_MODEL), F32)
    for j in range(D_FF // FF_CHUNK):
        gate = conv(slice(j * FF_CHUNK, (j + 1) * FF_CHUNK))
        val = conv(slice(D_FF + j * FF_CHUNK, D_FF + (j + 1) * FF_CHUNK))
        act = (gate * jax.nn.sigmoid(gate) * val).astype(BF16)
        acc = acc + _dot(act, wdn_ref[j * FF_CHUNK:(j + 1) * FF_CHUNK, :])
    y_ref[...] = _rms(x1 + gtf_ref[0] * acc, gfin_ref[...])
    conv_ref[0] = prev_s[0:2, :]


def _ffn_prompt(x2d, o_r, o_f, mod3, conv0, fp, nb, t_len):
    tm = ROW_TILE
    tpb = t_len // tm
    rows = nb * t_len
    row_spec = lambda n: pl.BlockSpec((tm, n), lambda i: (i, 0))
    mod_spec = lambda j: pl.BlockSpec((1, 1, D_MODEL), lambda i: (i // tpb, 0, j))
    const = lambda shape: pl.BlockSpec(shape, lambda i: (0, 0))
    conv_spec = pl.BlockSpec((1, CONV_W - 1, 2 * D_FF), lambda i: (i // tpb, 0, 0))
    return pl.pallas_call(
        functools.partial(_ffn_prompt_kernel, tpb), grid=(rows // tm,),
        in_specs=[row_spec(D_MODEL), row_spec(D_GROUP), row_spec(D_GROUP),
                  mod_spec(2), mod_spec(3), mod_spec(4), mod_spec(5),
                  const((D_MODEL, D_MODEL)), const((1, D_MODEL)), const((D_MODEL, 2 * D_FF)),
                  const((CONV_W, 2 * D_FF)), const((1, 2 * D_FF)), const((D_FF, D_MODEL)),
                  const((1, D_MODEL)), conv_spec],
        out_specs=[row_spec(D_MODEL), conv_spec],
        out_shape=[jax.ShapeDtypeStruct((rows, D_MODEL), F32),
                   jax.ShapeDtypeStruct((nb, CONV_W - 1, 2 * D_FF), F32)],
        scratch_shapes=[pltpu.VMEM((8, 2 * D_FF), F32)],
        compiler_params=_params(dimension_semantics=("arbitrary",)), name="ffn_prompt")(
            x2d, o_r, o_f, mod3, mod3, mod3, mod3, fp["w_out"], fp["g_ffn"], fp["w_up"], fp["conv_w"],
            fp["conv_b"], fp["w_down"], fp["g_final"], conv0)


def _ffn_sample_kernel(x_ref, or_ref, of_ref, gta_ref, shf_ref, scf_ref, gtf_ref, wout_ref, gffn_ref,
                       wup_ref, cw_ref, cb_ref, wdn_ref, gfin_ref, cp0_ref, cp1_ref, y_ref, u_ref):
    x1, h2 = _ffn_front(x_ref[...], or_ref[...], of_ref[...], gta_ref[...], shf_ref[...], scf_ref[...],
                        wout_ref, gffn_ref[...])

    def conv(cols):
        u = _dot(h2, wup_ref[:, cols])
        u_ref[:, cols] = u
        return (cb_ref[:, cols] + cw_ref[0:1, cols] * cp0_ref[:, cols] + cw_ref[1:2, cols] * cp1_ref[:, cols]
                + cw_ref[2:3, cols] * u)

    acc = jnp.zeros(x1.shape, F32)
    for j in range(D_FF // FF_CHUNK):
        gate = conv(slice(j * FF_CHUNK, (j + 1) * FF_CHUNK))
        val = conv(slice(D_FF + j * FF_CHUNK, D_FF + (j + 1) * FF_CHUNK))
        act = (gate * jax.nn.sigmoid(gate) * val).astype(BF16)
        acc = acc + _dot(act, wdn_ref[j * FF_CHUNK:(j + 1) * FF_CHUNK, :])
    y_ref[...] = _rms(x1 + gtf_ref[...] * acc, gfin_ref[...])


def _ffn_sample(x2d, o_r, o_f, mod, cp0, cp1, fp):
    m = x2d.shape[0]
    full = lambda n: pl.BlockSpec((m, n), lambda i: (0, 0))
    mod_spec = lambda j: pl.BlockSpec((m, D_MODEL), lambda i: (0, j))
    const = lambda shape: pl.BlockSpec(shape, lambda i: (0, 0))
    return pl.pallas_call(
        _ffn_sample_kernel, grid=(1,),
        in_specs=[full(D_MODEL), full(D_GROUP), full(D_GROUP),
                  mod_spec(2), mod_spec(3), mod_spec(4), mod_spec(5),
                  const((D_MODEL, D_MODEL)), const((1, D_MODEL)), const((D_MODEL, 2 * D_FF)),
                  const((CONV_W, 2 * D_FF)), const((1, 2 * D_FF)), const((D_FF, D_MODEL)),
                  const((1, D_MODEL)), full(2 * D_FF), full(2 * D_FF)],
        out_specs=[full(D_MODEL), full(2 * D_FF)],
        out_shape=[jax.ShapeDtypeStruct((m, D_MODEL), F32), jax.ShapeDtypeStruct((m, 2 * D_FF), F32)],
        compiler_params=_params(), name="ffn_sample")(
            x2d, o_r, o_f, mod, mod, mod, mod, fp["w_out"], fp["g_ffn"], fp["w_up"], fp["conv_w"],
            fp["conv_b"], fp["w_down"], fp["g_final"], cp0, cp1)


def _rwkv_prep_kernel(z_ref, sh_ref, mu_ref, w0_ref, wup_ref, a0_ref, aup_ref, gup_ref, kk_ref, ka_ref,
                      r_ref, k_ref, v_ref, d_ref, kko_ref, a_ref, g_ref):
    z = z_ref[...]
    zm = z + mu_ref[...] * (sh_ref[...] - z)
    r, k, v, logd, kk, a, g = _rwkv_mix(zm, w0_ref[...], wup_ref[...], a0_ref[...], aup_ref[...],
                                        gup_ref[...], kk_ref[...], ka_ref[...])
    r_ref[...] = r
    k_ref[...] = k
    v_ref[...] = v
    d_ref[...] = jnp.exp(logd)
    kko_ref[...] = kk
    a_ref[...] = a
    g_ref[...] = g


def _rwkv_prep(zr, shift, rp):
    m = zr.shape[0]
    return pl.pallas_call(
        _rwkv_prep_kernel,
        out_shape=[jax.ShapeDtypeStruct((m, D_GROUP), F32)] * 7,
        compiler_params=_params(), name="rwkv_prep")(
            zr, shift, rp["mu"], rp["w0"], rp["wup"], rp["a0"], rp["aup"], rp["gup"], rp["k_k"], rp["k_a"])


def _rwkv_step_kernel(r_ref, k_ref, v_ref, d_ref, kk_ref, a_ref, g_ref, rk_ref, lnw_ref, lnb_ref, s_ref,
                      o_ref, so_ref):
    n = HEAD_DIM
    ii = lax.broadcasted_iota(jnp.int32, (n, n), 0)
    jj = lax.broadcasted_iota(jnp.int32, (n, n), 1)
    eye = ii == jj
    rows = []
    for h in range(N_HEADS):
        hs = slice(h, h + 1)
        r, k, v, d, kk, a = (ref[0, hs, :] for ref in (r_ref, k_ref, v_ref, d_ref, kk_ref, a_ref))
        nrm = jnp.sqrt(jnp.sum(kk * kk, axis=-1, keepdims=True))
        kkn = kk / jnp.maximum(nrm, 1e-12)
        s = s_ref[h]
        sa = -jnp.sum(s * kkn, axis=-1, keepdims=True)
        vcol = jnp.sum(jnp.where(eye, v, 0.0), axis=-1, keepdims=True)
        s = s * d + sa * (kkn * a) + vcol * k
        so_ref[h] = s
        ycol = jnp.sum(s * r, axis=-1, keepdims=True)
        y = jnp.sum(jnp.where(eye, ycol, 0.0), axis=0, keepdims=True)
        bonus = jnp.sum(r * k * rk_ref[hs, :], axis=-1, keepdims=True) * v
        rows.append((_group_norm(y) * lnw_ref[hs, :] + lnb_ref[hs, :] + bonus) * g_ref[0, hs, :])
    o_ref[0] = jnp.concatenate(rows, axis=0)


def _rwkv_step(vecs, g3, rp, state):
    nb = state.shape[0] // N_HEADS
    vec_spec = pl.BlockSpec((1, N_HEADS, HEAD_DIM), lambda b: (b, 0, 0))
    par_spec = pl.BlockSpec((N_HEADS, HEAD_DIM), lambda b: (0, 0))
    st_spec = pl.BlockSpec((N_HEADS, HEAD_DIM, HEAD_DIM), lambda b: (b, 0, 0))
    return pl.pallas_call(
        _rwkv_step_kernel, grid=(nb,),
        in_specs=[vec_spec] * 7 + [par_spec] * 3 + [st_spec],
        out_specs=[vec_spec, st_spec],
        out_shape=[jax.ShapeDtypeStruct((nb, N_HEADS, HEAD_DIM), F32),
                   jax.ShapeDtypeStruct(state.shape, F32)],
        compiler_params=_params(dimension_semantics=("parallel",)), name="rwkv_step")(
            *vecs, g3, rp["r_k2"], rp["ln_w2"], rp["ln_b2"], state)


def _lane_scan(x, backward):
    width = x.shape[1]
    lane = lax.broadcasted_iota(jnp.int32, x.shape, 1)
    s = N_HEADS
    while s < width:
        if backward:
            x = x + pltpu.roll(x, width - s, axis=1)
        else:
            x = x + jnp.where(lane >= s, pltpu.roll(x, s, axis=1), 0.0)
        s *= 2
    return x


def _pattn_kernel(npg, pt_ref, q_ref, kn_ref, vn_ref, lfn_ref, *refs):
    del pt_ref
    pg = PAGES_PER_STEP
    k_refs, v_refs, lf_refs = refs[0:pg], refs[pg:2 * pg], refs[2 * pg:3 * pg]
    o_ref, m_s, l_s, acc_s, c_s = refs[3 * pg:]
    g = pl.program_id(1)
    width = PAGE * N_HEADS

    @pl.when(g == 0)
    def _():
        m_s[...] = jnp.full_like(m_s, -jnp.inf)
        l_s[...] = jnp.zeros_like(l_s)
        acc_s[...] = jnp.zeros_like(acc_s)
        c_s[...] = jnp.zeros_like(c_s)

    q = q_ref[0]
    lf = jnp.concatenate([lf_refs[p][0] for p in range(pg)], axis=0)
    within = _lane_scan(lf, backward=False)
    lane = lax.broadcasted_iota(jnp.int32, (pg, width), 1)
    totals = _lane_scan(jnp.where(lane >= width - N_HEADS, within, 0.0), backward=True)
    sub = lax.broadcasted_iota(jnp.int32, (N_HEADS, width), 0)
    lane8 = lax.broadcasted_iota(jnp.int32, (N_HEADS, width), 1)
    own = (lane8 % N_HEADS) == sub
    run = c_s[...]
    logits = []
    for p in range(pg):
        kp = k_refs[p][0, 0].reshape(width, HEAD_DIM).astype(BF16)
        s = _dot_nt(q, kp) - (run + within[p:p + 1, :])
        logits.append(jnp.where(own, s, -jnp.inf))
        run = run + totals[p:p + 1, :]
    c_s[...] = run
    m_old = m_s[:, 0:1]
    m_new = m_old
    for s in logits:
        m_new = jnp.maximum(m_new, jnp.max(s, axis=-1, keepdims=True))
    alpha = jnp.exp(m_old - m_new)
    l = alpha * l_s[:, 0:1]
    acc = alpha * acc_s[...]
    for p in range(pg):
        pr = jnp.exp(logits[p] - m_new)
        l = l + jnp.sum(pr, axis=-1, keepdims=True)
        vp = v_refs[p][0, 0].reshape(width, HEAD_DIM).astype(BF16)
        acc = acc + _dot(pr.astype(BF16), vp)
    m_s[...] = jnp.broadcast_to(m_new, m_s.shape)
    l_s[...] = jnp.broadcast_to(l, l_s.shape)
    acc_s[...] = acc

    @pl.when(g == npg - 1)
    def _():
        r8 = lax.broadcasted_iota(jnp.int32, (N_HEADS, N_HEADS), 0)
        c8 = lax.broadcasted_iota(jnp.int32, (N_HEADS, N_HEADS), 1)
        to_col = lambda rowv: jnp.sum(jnp.where(r8 == c8, rowv, 0.0), axis=-1, keepdims=True)
        c_new = to_col(run[:, 0:N_HEADS]) + to_col(lfn_ref[0])
        s_new = jnp.sum(q.astype(F32) * kn_ref[0], axis=-1, keepdims=True) - c_new
        m_fin = jnp.maximum(m_new, s_new)
        beta = jnp.exp(m_new - m_fin)
        p_new = jnp.exp(s_new - m_fin)
        o_ref[0] = (acc * beta + p_new * vn_ref[0]) / (l * beta + p_new)


def _attn_sample(qb3, kn3, vn3, lfn3, cache_k, cache_v, lf_flat, page_table):
    nb, n_pages = page_table.shape
    pg = PAGES_PER_STEP
    npg = n_pages // pg
    width = PAGE * N_HEADS
    vec_spec = pl.BlockSpec((1, N_HEADS, HEAD_DIM), lambda b, g, pt: (b, 0, 0))
    page_spec = lambda p: pl.BlockSpec((1, 1, PAGE, N_HEADS, HEAD_DIM),
                                       lambda b, g, pt: (0, pt[b, g * pg + p], 0, 0, 0))
    lf_spec = lambda p: pl.BlockSpec((1, 1, width), lambda b, g, pt: (pt[b, g * pg + p], 0, 0))
    grid_spec = pltpu.PrefetchScalarGridSpec(
        num_scalar_prefetch=1, grid=(nb, npg),
        in_specs=[vec_spec, vec_spec, vec_spec, pl.BlockSpec((1, 1, N_HEADS), lambda b, g, pt: (b, 0, 0))]
        + [page_spec(p) for p in range(pg)] + [page_spec(p) for p in range(pg)]
        + [lf_spec(p) for p in range(pg)],
        out_specs=vec_spec,
        scratch_shapes=[pltpu.VMEM((N_HEADS, LANES), F32), pltpu.VMEM((N_HEADS, LANES), F32),
                        pltpu.VMEM((N_HEADS, HEAD_DIM), F32), pltpu.VMEM((1, width), F32)])
    return pl.pallas_call(
        functools.partial(_pattn_kernel, npg), grid_spec=grid_spec,
        out_shape=jax.ShapeDtypeStruct((nb, N_HEADS, HEAD_DIM), F32),
        compiler_params=_params(dimension_semantics=("parallel", "arbitrary")), name="fox_sample")(
            page_table, qb3, kn3, vn3, lfn3, *([cache_k] * pg), *([cache_v] * pg), *([lf_flat] * pg))


def _pad_cols(a, n):
    return jnp.pad(a, ((0, 0), (0, n - a.shape[1])))


def kernel(x_prompt, x_sample, c_prompt, c_sample, cache_k, cache_v, cache_logf, page_table, state_wkv,
           state_shift, state_ffn_conv, w_ada, b_ada, g_attn_norm, w_in, b_forget, rwkv_mu, rwkv_w0,
           rwkv_w_up, rwkv_a0, rwkv_a_up, rwkv_g_up, rwkv_k_k, rwkv_k_a, rwkv_r_k, rwkv_ln_w, rwkv_ln_b,
           w_out, g_ffn_norm, w_ffn_up, ffn_conv_w, ffn_conv_b, w_ffn_down, g_final_norm):
    depth = w_in.shape[0]
    assert depth == 1, "one layer per call"
    nb, t_len, _ = x_prompt.shape
    db, dt, _ = x_sample.shape
    assert dt == 1, "the sample group decodes one token per sequence"
    assert t_len % ATTN_TILE == 0 and db % 8 == 0

    w_ada_bf = w_ada[0].astype(BF16)
    b_ada2 = b_ada[0][None, :]
    wi = w_in[0]
    o = D_RWKV_IN
    w_in_bf = jnp.concatenate(
        [_pad_cols(wi[:, :o], ZR_PAD), wi[:, o:o + 3 * D_GROUP], _pad_cols(wi[:, o + 3 * D_GROUP:], F_PAD)],
        axis=1).astype(BF16)
    bfp = _pad_cols(b_forget[0][None, :], F_PAD)
    g_attn = g_attn_norm[0][None, :]
    lora_rows = lambda w, off: jnp.pad(w, ((off, LORA_W - off - w.shape[0]), (0, 0))).astype(BF16)
    row = lambda a: a[None, :]
    rp = dict(mu=_pad_cols(row(rwkv_mu[0]), ZR_PAD), w0=row(rwkv_w0[0]), a0=row(rwkv_a0[0]),
              wup=lora_rows(rwkv_w_up[0], 0), aup=lora_rows(rwkv_a_up[0], W_LORA),
              gup=lora_rows(rwkv_g_up[0], W_LORA + A_LORA),
              k_k=row(rwkv_k_k[0]), k_a=row(rwkv_k_a[0]), ln_w=row(rwkv_ln_w[0]), ln_b=row(rwkv_ln_b[0]),
              r_k3=rwkv_r_k[0][:, None, :], r_k2=rwkv_r_k[0],
              ln_w2=rwkv_ln_w[0].reshape(N_HEADS, HEAD_DIM), ln_b2=rwkv_ln_b[0].reshape(N_HEADS, HEAD_DIM))
    fp = dict(w_out=w_out[0].astype(BF16), g_ffn=row(g_ffn_norm[0]), w_up=w_ffn_up[0].astype(BF16),
              conv_w=ffn_conv_w[0], conv_b=row(ffn_conv_b[0]), w_down=w_ffn_down[0].astype(BF16),
              g_final=row(g_final_norm))

    rows = nb * t_len
    xp = x_prompt.reshape(rows, D_MODEL)
    mod_p = _mod(jnp.pad(c_prompt, ((0, 8 - nb), (0, 0))), w_ada_bf, b_ada2).reshape(8, 1, 6 * D_MODEL)
    zr, qb, k_p, v_p, kb, vb, lf_p, lft = _inproj(xp, mod_p, False, t_len // ROW_TILE, g_attn, w_in_bf, bfp,
                                                  ROW_TILE)
    cum = _cumsum(lft, nb, t_len)
    o_r, st = _rwkv_prompt(zr, jnp.zeros((nb, 1, ZR_PAD), F32), nb, t_len, rp)
    o_f = _attn_prompt(qb, kb, vb, cum, nb, t_len)
    y_p, conv_p = _ffn_prompt(xp, o_r, o_f, mod_p, jnp.zeros((nb, CONV_W - 1, 2 * D_FF), F32), fp, nb, t_len)

    y_prompt = y_p.reshape(nb, t_len, D_MODEL)
    k_prompt = k_p.reshape(1, nb, t_len, N_HEADS, HEAD_DIM)
    v_prompt = v_p.reshape(1, nb, t_len, N_HEADS, HEAD_DIM)
    logf_prompt = lf_p.reshape(1, nb, t_len, N_HEADS)
    wkv_prompt = jnp.swapaxes(st, -1, -2).reshape(1, nb, N_HEADS, HEAD_DIM, HEAD_DIM)
    shift_prompt = zr.reshape(nb, t_len, ZR_PAD)[:, -1, :D_RWKV_IN][None]
    conv_prompt = conv_p[None]

    xs = x_sample.reshape(db, D_MODEL)
    mod_s = _mod(c_sample, w_ada_bf, b_ada2)
    zr_s, qb_s, k_s, v_s, _, _, lf_s, _ = _inproj(xs, mod_s, True, 1, g_attn, w_in_bf, bfp, db)
    vecs = _rwkv_prep(zr_s, _pad_cols(state_shift[0], ZR_PAD), rp)
    to3 = lambda a: a.reshape(db, N_HEADS, HEAD_DIM)
    o_r_s, wkv_s = _rwkv_step([to3(a) for a in vecs[:6]], to3(vecs[6]), rp,
                              state_wkv[0].reshape(db * N_HEADS, HEAD_DIM, HEAD_DIM))
    n_phys = cache_logf.shape[1]
    lf_flat = cache_logf[0].reshape(n_phys, 1, PAGE * N_HEADS)
    o_f_s = _attn_sample(to3(qb_s), to3(k_s), to3(v_s), lf_s.reshape(db, 1, N_HEADS), cache_k, cache_v,
                         lf_flat, page_table)
    cp = state_ffn_conv[0]
    y_s, u_s = _ffn_sample(xs, o_r_s.reshape(db, D_GROUP).astype(BF16), o_f_s.reshape(db, D_GROUP).astype(BF16),
                           mod_s, cp[:, 0, :], cp[:, 1, :], fp)

    y_sample = y_s.reshape(db, 1, D_MODEL)
    k_sample = k_s.reshape(1, db, 1, N_HEADS, HEAD_DIM)
    v_sample = v_s.reshape(1, db, 1, N_HEADS, HEAD_DIM)
    logf_sample = lf_s.reshape(1, db, 1, N_HEADS)
    wkv_sample = wkv_s.reshape(1, db, N_HEADS, HEAD_DIM, HEAD_DIM)
    shift_sample = zr_s[:, :D_RWKV_IN][None]
    conv_sample = jnp.stack([cp[:, 1, :], u_s], axis=1)[None]
    return (y_prompt, y_sample, k_prompt, v_prompt, logf_prompt, wkv_prompt, shift_prompt, conv_prompt,
            k_sample, v_sample, logf_sample, wkv_sample, shift_sample, conv_sample)
```

```python
import functools

import jax
import jax.numpy as jnp
from jax import lax
from jax.experimental import pallas as pl
from jax.experimental.pallas import tpu as pltpu

F32 = jnp.float32
BF16 = jnp.bfloat16

D_MODEL = 1024
HEAD_DIM = 64
N_HEADS = 8
D_GROUP = N_HEADS * HEAD_DIM
W_LORA, A_LORA, G_LORA = 32, 32, 96
D_RWKV_IN = 3 * D_GROUP + W_LORA + A_LORA + G_LORA
ZR_PAD = 1792
LORA_OFF = 3 * D_GROUP
LORA_W = ZR_PAD - LORA_OFF
F_PAD = 128
W_IN_PAD = ZR_PAD + 3 * D_GROUP + F_PAD
D_FF = 2816
CONV_W = 3
PAGE = 128
RMS_EPS = 1e-6
GN_EPS = 64e-5
ATTN_SCALE = HEAD_DIM ** -0.5
LANES = 128
VMEM_LIMIT = 56 * 1024 * 1024

ROW_TILE = 256
RWKV_TILE = 256
CHUNK = 64
ATTN_TILE = 512
FF_CHUNK = 256
PAGES_PER_STEP = 16


def _params(**kw):
    return pltpu.CompilerParams(vmem_limit_bytes=VMEM_LIMIT, **kw)


def _dot(a, b):
    return jnp.dot(a, b, preferred_element_type=F32)


def _dot_nt(a, b):
    return lax.dot_general(a, b, (((1,), (1,)), ((), ())), preferred_element_type=F32)


def _dot_tn(a, b):
    return lax.dot_general(a, b, (((0,), (0,)), ((), ())), preferred_element_type=F32)


def _bdot(a, b):
    return lax.dot_general(a, b, (((2,), (1,)), ((0,), (0,))), preferred_element_type=F32)


def _bdot_nt(a, b):
    return lax.dot_general(a, b, (((2,), (2,)), ((0,), (0,))), preferred_element_type=F32)


def _bdot_tn(a, b):
    return lax.dot_general(a, b, (((1,), (1,)), ((0,), (0,))), preferred_element_type=F32)


def _split3(x):
    hi = x.astype(BF16)
    r1 = x - hi.astype(F32)
    mid = r1.astype(BF16)
    lo = (r1 - mid.astype(F32)).astype(BF16)
    return hi, mid, lo


def _rms(x, g):
    return x * lax.rsqrt(jnp.mean(x * x, axis=-1, keepdims=True) + RMS_EPS) * g


def _mod_kernel(c_ref, w_ref, b_ref, o_ref):
    c = c_ref[...]
    s = (c * jax.nn.sigmoid(c)).astype(BF16)
    o_ref[...] = _dot(s, w_ref[...]) + b_ref[...]


def _mod(c, w_bf, b):
    m, n, tn = c.shape[0], w_bf.shape[1], 1536
    return pl.pallas_call(
        _mod_kernel, grid=(n // tn,),
        in_specs=[pl.BlockSpec((m, D_MODEL), lambda j: (0, 0)),
                  pl.BlockSpec((D_MODEL, tn), lambda j: (0, j)),
                  pl.BlockSpec((1, tn), lambda j: (0, j))],
        out_specs=pl.BlockSpec((m, tn), lambda j: (0, j)),
        out_shape=jax.ShapeDtypeStruct((m, n), F32),
        compiler_params=_params(), name="adaln_mod")(c, w_bf, b)


def _inproj_kernel(per_row, x_ref, sh_ref, sc_ref, g_ref, w_ref, bf_ref,
                   zr_ref, qb_ref, k_ref, v_ref, kb_ref, vb_ref, lf_ref, lft_ref):
    sh = sh_ref[...] if per_row else sh_ref[0]
    sc = sc_ref[...] if per_row else sc_ref[0]
    h = (_rms(x_ref[...], g_ref[...]) * (1 + sc) + sh).astype(BF16)
    zr_ref[...] = _dot(h, w_ref[:, 0:ZR_PAD])
    o = ZR_PAD
    qb_ref[...] = (_dot(h, w_ref[:, o:o + D_GROUP]) * ATTN_SCALE).astype(BF16)
    k = _dot(h, w_ref[:, o + D_GROUP:o + 2 * D_GROUP])
    k_ref[...] = k
    kb_ref[...] = k.astype(BF16)
    v = _dot(h, w_ref[:, o + 2 * D_GROUP:o + 3 * D_GROUP])
    v_ref[...] = v
    vb_ref[...] = v.astype(BF16)
    f = _dot(h, w_ref[:, o + 3 * D_GROUP:o + 3 * D_GROUP + F_PAD])
    lf = jax.nn.log_sigmoid(f + bf_ref[...])
    lf_ref[...] = lf[:, 0:N_HEADS]
    lft_ref[...] = lf.T[0:N_HEADS, :]


def _inproj(x2d, mod, per_row, tiles_per_b, g, w_bf, bfp, tm):
    rows = x2d.shape[0]
    if per_row:
        sh_spec = pl.BlockSpec((tm, D_MODEL), lambda i: (i, 0))
        sc_spec = pl.BlockSpec((tm, D_MODEL), lambda i: (i, 1))
    else:
        sh_spec = pl.BlockSpec((1, 1, D_MODEL), lambda i: (i // tiles_per_b, 0, 0))
        sc_spec = pl.BlockSpec((1, 1, D_MODEL), lambda i: (i // tiles_per_b, 0, 1))
    row_spec = lambda n: pl.BlockSpec((tm, n), lambda i: (i, 0))
    const = lambda shape: pl.BlockSpec(shape, lambda i: (0, 0))
    return pl.pallas_call(
        functools.partial(_inproj_kernel, per_row), grid=(rows // tm,),
        in_specs=[row_spec(D_MODEL), sh_spec, sc_spec, const((1, D_MODEL)),
                  const((D_MODEL, W_IN_PAD)), const((1, F_PAD))],
        out_specs=[row_spec(ZR_PAD), row_spec(D_GROUP), row_spec(D_GROUP), row_spec(D_GROUP),
                   row_spec(D_GROUP), row_spec(D_GROUP), row_spec(N_HEADS),
                   pl.BlockSpec((N_HEADS, tm), lambda i: (0, i))],
        out_shape=[jax.ShapeDtypeStruct((rows, ZR_PAD), F32),
                   jax.ShapeDtypeStruct((rows, D_GROUP), BF16),
                   jax.ShapeDtypeStruct((rows, D_GROUP), F32),
                   jax.ShapeDtypeStruct((rows, D_GROUP), F32),
                   jax.ShapeDtypeStruct((rows, D_GROUP), BF16),
                   jax.ShapeDtypeStruct((rows, D_GROUP), BF16),
                   jax.ShapeDtypeStruct((rows, N_HEADS), F32),
                   jax.ShapeDtypeStruct((N_HEADS, rows), F32)],
        compiler_params=_params(dimension_semantics=("parallel",)), name="in_proj")(
            x2d, mod, mod, g, w_bf, bfp)


def _cumsum_kernel(nb, t_len, x_ref, o_ref):
    blk = 256
    r = lax.broadcasted_iota(jnp.int32, (blk, blk), 0)
    c = lax.broadcasted_iota(jnp.int32, (blk, blk), 1)
    upper = (r <= c).astype(BF16)
    for b in range(nb):
        carry = jnp.zeros((N_HEADS, 1), F32)
        for j in range(t_len // blk):
            sl = slice(b * t_len + j * blk, b * t_len + (j + 1) * blk)
            hi, mid, lo = _split3(x_ref[:, sl])
            cs = (_dot(lo, upper) + _dot(mid, upper)) + _dot(hi, upper) + carry
            o_ref[:, sl] = cs
            carry = cs[:, blk - 1:blk]


def _cumsum(lft, nb, t_len):
    return pl.pallas_call(
        functools.partial(_cumsum_kernel, nb, t_len),
        out_shape=jax.ShapeDtypeStruct(lft.shape, F32),
        compiler_params=_params(), name="logf_cumsum")(lft)


def _rwkv_mix(zm, w0, wup, a0, aup, gup, k_k, k_a):
    r = zm[:, 0:D_GROUP]
    k = zm[:, D_GROUP:2 * D_GROUP]
    v = zm[:, 2 * D_GROUP:3 * D_GROUP]
    lora = zm[:, LORA_OFF:ZR_PAD]
    w = w0 + _dot(jnp.tanh(lora).astype(BF16), wup)
    logd = -jnp.exp(-jax.nn.softplus(-w) - 0.5)
    a = jax.nn.sigmoid(a0 + _dot(lora.astype(BF16), aup))
    g = _dot(jax.nn.sigmoid(lora).astype(BF16), gup)
    kk = k * k_k
    k = k * (1 + (a - 1) * k_a)
    return r, k, v, logd, kk, a, g


def _group_norm(y):
    mean = jnp.mean(y, axis=-1, keepdims=True)
    var = jnp.mean(jnp.square(y - mean), axis=-1, keepdims=True)
    return (y - mean) * lax.rsqrt(var + GN_EPS)


def _rwkv_kernel(z_ref, sh0_ref, mu_ref, w0_ref, wup_ref, a0_ref, aup_ref, gup_ref, kk_ref, ka_ref,
                 rk_ref, lnw_ref, lnb_ref, o_ref, st_ref, prev_s, st_s, hm_s, yn_s, bon_s):
    tt, c = RWKV_TILE, CHUNK
    t = pl.program_id(1)

    @pl.when(t == 0)
    def _():
        prev_s[0:1, :] = sh0_ref[0]
        st_s[...] = jnp.zeros_like(st_s)

    z = z_ref[...]
    row = lax.broadcasted_iota(jnp.int32, (tt, 1), 0)
    zprev = jnp.where(row == 0, prev_s[0:1, :], pltpu.roll(z, 1, axis=0))
    prev_s[0:1, :] = z[tt - 1:tt, :]
    zm = z + mu_ref[...] * (zprev - z)
    r, k, v, logd, kk, a, g = _rwkv_mix(zm, w0_ref[...], wup_ref[...], a0_ref[...], aup_ref[...],
                                        gup_ref[...], kk_ref[...], ka_ref[...])
    ri = lax.broadcasted_iota(jnp.int32, (tt, tt), 0)
    ci = lax.broadcasted_iota(jnp.int32, (tt, tt), 1)
    tri = ((ri // c == ci // c) & (ci <= ri)).astype(BF16)
    hi, mid, lo = _split3(logd)
    cl = (_dot(tri, lo) + _dot(tri, mid)) + _dot(tri, hi)
    for h in range(N_HEADS):
        sl = slice(h * HEAD_DIM, (h + 1) * HEAD_DIM)
        for n, val in enumerate((r, k, v, kk, a, logd, cl)):
            hm_s[n, h] = val[:, sl]

    rr = lax.broadcasted_iota(jnp.int32, (c, c), 0)
    cc = lax.broadcasted_iota(jnp.int32, (c, c), 1)
    strict = (cc < rr).astype(F32)
    eye = rr == cc
    r2 = lax.broadcasted_iota(jnp.int32, (c, 2 * c), 0)
    c2 = lax.broadcasted_iota(jnp.int32, (c, 2 * c), 1)
    sgn = jnp.where(c2 < c, jnp.where(c2 <= r2, 1.0, 0.0), jnp.where(c2 - c <= r2, -1.0, 0.0)).astype(F32)
    nch = tt // c
    grp = N_HEADS * nch
    r_, k_, v_, kk_, a_, ld_, cl_ = (hm_s[n].reshape(grp, c, HEAD_DIM) for n in range(7))
    nrm = jnp.sqrt(jnp.sum(kk_ * kk_, axis=-1, keepdims=True))
    kkn = kk_ / jnp.maximum(nrm, 1e-12)
    b_ = kkn * a_
    cl_end = cl_[:, c - 1:c, :]
    e_end = jnp.exp(cl_end - cl_)
    g_inv = jnp.exp(-cl_)
    kkg = kkn * jnp.exp(cl_ - ld_)
    rg = r_ * jnp.exp(cl_)
    kd = (k_ * g_inv).astype(BF16)
    bd = (b_ * g_inv).astype(BF16)
    kkg_b = kkg.astype(BF16)
    a_k = _bdot_nt(kkg_b, kd) * strict
    a_b = _bdot_nt(kkg_b, bd) * strict
    ll = _bdot_nt(rg.astype(BF16), jnp.concatenate([kd, bd], axis=1)) * sgn
    akv = _bdot(a_k.astype(BF16), v_.astype(BF16))
    x = jnp.concatenate([akv, kkg], axis=2)
    p = (-a_b).astype(BF16)
    x = x + _bdot(p, x.astype(BF16))
    n = 2
    while n < c:
        p = _bdot(p, p).astype(BF16)
        x = x + _bdot(p, x.astype(BF16))
        n *= 2
    rhs = jnp.concatenate([jnp.concatenate([v_, jnp.zeros_like(v_)], axis=2), x], axis=1).astype(BF16)
    yq = _bdot(ll.astype(BF16), rhs)
    kb = jnp.concatenate([k_ * e_end, -(b_ * e_end)], axis=1).astype(BF16)
    sw = _bdot_tn(kb, rhs)
    qc = (rg + yq[:, :, HEAD_DIM:]).astype(BF16)
    m = (jnp.where(eye, jnp.broadcast_to(jnp.exp(cl_end), (grp, c, c)), 0.0) + sw[:, :, HEAD_DIM:]).astype(BF16)
    by_head = lambda t: t.reshape(N_HEADS, nch, *t.shape[1:])
    qc, m, y0, w = by_head(qc), by_head(m), by_head(yq[:, :, 0:HEAD_DIM]), by_head(sw[:, :, 0:HEAD_DIM])
    st = st_s[...]
    ys = []
    for i in range(nch):
        st_b = st.astype(BF16)
        ys.append(_bdot(qc[:, i], st_b) + y0[:, i])
        st = _bdot(m[:, i], st_b) + w[:, i]
    st_s[...] = st
    yn_s[...] = _group_norm(jnp.concatenate(ys, axis=1))
    bon_s[...] = jnp.sum(hm_s[0] * hm_s[1] * rk_ref[...], axis=-1, keepdims=True) * hm_s[2]
    yn = jnp.concatenate([yn_s[h] for h in range(N_HEADS)], axis=1)
    bon = jnp.concatenate([bon_s[h] for h in range(N_HEADS)], axis=1)
    o_ref[...] = ((yn * lnw_ref[...] + lnb_ref[...] + bon) * g).astype(BF16)
    st_ref[...] = st_s[...]


def _rwkv_prompt(zr, shift0, nb, t_len, rp):
    tt = RWKV_TILE
    nt = t_len // tt
    const = lambda shape: pl.BlockSpec(shape, lambda b, t: (0,) * len(shape))
    return pl.pallas_call(
        _rwkv_kernel, grid=(nb, nt),
        in_specs=[pl.BlockSpec((tt, ZR_PAD), lambda b, t: (b * nt + t, 0)),
                  pl.BlockSpec((1, 1, ZR_PAD), lambda b, t: (b, 0, 0)),
                  const((1, ZR_PAD)), const((1, D_GROUP)), const((LORA_W, D_GROUP)),
                  const((1, D_GROUP)), const((LORA_W, D_GROUP)), const((LORA_W, D_GROUP)),
                  const((1, D_GROUP)), const((1, D_GROUP)), const((N_HEADS, 1, HEAD_DIM)),
                  const((1, D_GROUP)), const((1, D_GROUP))],
        out_specs=[pl.BlockSpec((tt, D_GROUP), lambda b, t: (b * nt + t, 0)),
                   pl.BlockSpec((N_HEADS, HEAD_DIM, HEAD_DIM), lambda b, t: (b, 0, 0))],
        out_shape=[jax.ShapeDtypeStruct((nb * t_len, D_GROUP), BF16),
                   jax.ShapeDtypeStruct((nb * N_HEADS, HEAD_DIM, HEAD_DIM), F32)],
        scratch_shapes=[pltpu.VMEM((8, ZR_PAD), F32),
                        pltpu.VMEM((N_HEADS, HEAD_DIM, HEAD_DIM), F32),
                        pltpu.VMEM((7, N_HEADS, tt, HEAD_DIM), F32),
                        pltpu.VMEM((N_HEADS, tt, HEAD_DIM), F32),
                        pltpu.VMEM((N_HEADS, tt, HEAD_DIM), F32)],
        compiler_params=_params(dimension_semantics=("arbitrary", "arbitrary")), name="rwkv_scan")(
            zr, shift0, rp["mu"], rp["w0"], rp["wup"], rp["a0"], rp["aup"], rp["gup"], rp["k_k"],
            rp["k_a"], rp["r_k3"], rp["ln_w"], rp["ln_b"])


def _attn_kernel(q_ref, k_ref, v_ref, c_ref, o_ref):
    tq = ATTN_TILE
    qi = pl.program_id(2)
    q2 = q_ref[...]
    lane = lax.broadcasted_iota(jnp.int32, (1, LANES), 1)
    lo = lane < HEAD_DIM
    zero = jnp.zeros_like(q2)
    q_h = (jnp.where(lo, q2, zero), jnp.where(lo, zero, q2))
    c_ref0 = c_ref[0, :, pl.ds(pl.multiple_of(qi * tq, tq), LANES)][:, 0:1]
    row = lax.broadcasted_iota(jnp.int32, (tq, tq), 0)
    col = lax.broadcasted_iota(jnp.int32, (tq, tq), 1)
    causal = col <= row

    def step(j, carry, masked):
        off = pl.multiple_of(j * tq, tq)
        kb = k_ref[pl.ds(off, tq), :]
        vb = v_ref[pl.ds(off, tq), :]
        ck = c_ref[0, :, pl.ds(off, tq)]
        out = []
        for hd in range(2):
            m, l, acc = carry[3 * hd:3 * hd + 3]
            s = _dot_nt(q_h[hd], kb) + (c_ref0[hd:hd + 1, :] - ck[hd:hd + 1, :])
            if masked:
                s = jnp.where(causal, s, -jnp.inf)
            m_new = jnp.maximum(m, jnp.max(s, axis=-1, keepdims=True))
            alpha = jnp.exp(m - m_new)
            p = jnp.exp(s - m_new)
            l = alpha * l + jnp.sum(p, axis=-1, keepdims=True)
            acc = alpha * acc + _dot(p.astype(BF16), vb)
            out += [m_new, l, acc]
        return tuple(out)

    init = (jnp.full((tq, 1), -jnp.inf, F32), jnp.zeros((tq, 1), F32), jnp.zeros((tq, LANES), F32)) * 2
    carry = lax.fori_loop(0, qi, lambda j, cr: step(j, cr, False), init)
    _, l0, acc0, _, l1, acc1 = step(qi, carry, True)
    o_ref[...] = jnp.where(lo, acc0 / l0, acc1 / l1).astype(BF16)


def _attn_prompt(qb, kb, vb, cum, nb, t_len):
    tq = ATTN_TILE
    nq = t_len // tq
    n_pairs = D_GROUP // LANES
    cum3 = cum.reshape(n_pairs, 2, nb * t_len)
    return pl.pallas_call(
        _attn_kernel, grid=(nb, n_pairs, nq),
        in_specs=[pl.BlockSpec((tq, LANES), lambda b, hp, qi: (b * nq + qi, hp)),
                  pl.BlockSpec((t_len, LANES), lambda b, hp, qi: (b, hp)),
                  pl.BlockSpec((t_len, LANES), lambda b, hp, qi: (b, hp)),
                  pl.BlockSpec((1, 2, t_len), lambda b, hp, qi: (hp, 0, b))],
        out_specs=pl.BlockSpec((tq, LANES), lambda b, hp, qi: (b * nq + qi, hp)),
        out_shape=jax.ShapeDtypeStruct((nb * t_len, D_GROUP), BF16),
        compiler_params=_params(dimension_semantics=("parallel", "parallel", "arbitrary")),
        name="fox_prompt")(qb, kb, vb, cum3)


def _ffn_front(x, o_r, o_f, gta, shf, scf, wout_ref, gffn):
    attn = _dot(o_r, wout_ref[0:D_GROUP, :]) + _dot(o_f, wout_ref[D_GROUP:2 * D_GROUP, :])
    x1 = x + gta * attn
    return x1, (_rms(x1, gffn) * (1 + scf) + shf).astype(BF16)


def _ffn_prompt_kernel(tpb, x_ref, or_ref, of_ref, gta_ref, shf_ref, scf_ref, gtf_ref, wout_ref, gffn_ref,
                       wup_ref, cw_ref, cb_ref, wdn_ref, gfin_ref, cp_ref, y_ref, conv_ref, prev_s):
    i = pl.program_id(0)

    @pl.when(i % tpb == 0)
    def _():
        prev_s[0:2, :] = cp_ref[0]

    x1, h2 = _ffn_front(x_ref[...], or_ref[...], of_ref[...], gta_ref[0], shf_ref[0], scf_ref[0],
                        wout_ref, gffn_ref[...])
    tm = x1.shape[0]
    row = lax.broadcasted_iota(jnp.int32, (tm, 1), 0)

    def conv(cols):
        u = _dot(h2, wup_ref[:, cols])
        p0 = prev_s[0:1, cols]
        p1 = prev_s[1:2, cols]
        u1 = jnp.where(row == 0, p1, pltpu.roll(u, 1, axis=0))
        u2 = jnp.where(row == 0, p0, jnp.where(row == 1, p1, pltpu.roll(u, 2, axis=0)))
        prev_s[0:2, cols] = u[tm - 2:tm, :]
        return cb_ref[:, cols] + cw_ref[0:1, cols] * u2 + cw_ref[1:2, cols] * u1 + cw_ref[2:3, cols] * u

    acc = jnp.zeros((tm, D_MODEL), F32)
    for j in range(D_FF // FF_CHUNK):
        gate = conv(slice(j * FF_CHUNK, (j + 1) * FF_CHUNK))
        val = conv(slice(D_FF + j * FF_CHUNK, D_FF + (j + 1) * FF_CHUNK))
        act = (gate * jax.nn.sigmoid(gate) * val).astype(BF16)
        acc = acc + _dot(act, wdn_ref[j * FF_CHUNK:(j + 1) * FF_CHUNK, :])
    y_ref[...] = _rms(x1 + gtf_ref[0] * acc, gfin_ref[...])
    conv_ref[0] = prev_s[0:2, :]


def _ffn_prompt(x2d, o_r, o_f, mod3, conv0, fp, nb, t_len):
    tm = ROW_TILE
    tpb = t_len // tm
    rows = nb * t_len
    row_spec = lambda n: pl.BlockSpec((tm, n), lambda i: (i, 0))
    mod_spec = lambda j: pl.BlockSpec((1, 1, D_MODEL), lambda i: (i // tpb, 0, j))
    const = lambda shape: pl.BlockSpec(shape, lambda i: (0, 0))
    conv_spec = pl.BlockSpec((1, CONV_W - 1, 2 * D_FF), lambda i: (i // tpb, 0, 0))
    return pl.pallas_call(
        functools.partial(_ffn_prompt_kernel, tpb), grid=(rows // tm,),
        in_specs=[row_spec(D_MODEL), row_spec(D_GROUP), row_spec(D_GROUP),
                  mod_spec(2), mod_spec(3), mod_spec(4), mod_spec(5),
                  const((D_MODEL, D_MODEL)), const((1, D_MODEL)), const((D_MODEL, 2 * D_FF)),
                  const((CONV_W, 2 * D_FF)), const((1, 2 * D_FF)), const((D_FF, D_MODEL)),
                  const((1, D_MODEL)), conv_spec],
        out_specs=[row_spec(D_MODEL), conv_spec],
        out_shape=[jax.ShapeDtypeStruct((rows, D_MODEL), F32),
                   jax.ShapeDtypeStruct((nb, CONV_W - 1, 2 * D_FF), F32)],
        scratch_shapes=[pltpu.VMEM((8, 2 * D_FF), F32)],
        compiler_params=_params(dimension_semantics=("arbitrary",)), name="ffn_prompt")(
            x2d, o_r, o_f, mod3, mod3, mod3, mod3, fp["w_out"], fp["g_ffn"], fp["w_up"], fp["conv_w"],
            fp["conv_b"], fp["w_down"], fp["g_final"], conv0)


def _ffn_sample_kernel(x_ref, or_ref, of_ref, gta_ref, shf_ref, scf_ref, gtf_ref, wout_ref, gffn_ref,
                       wup_ref, cw_ref, cb_ref, wdn_ref, gfin_ref, cp0_ref, cp1_ref, y_ref, u_ref):
    x1, h2 = _ffn_front(x_ref[...], or_ref[...], of_ref[...], gta_ref[...], shf_ref[...], scf_ref[...],
                        wout_ref, gffn_ref[...])

    def conv(cols):
        u = _dot(h2, wup_ref[:, cols])
        u_ref[:, cols] = u
        return (cb_ref[:, cols] + cw_ref[0:1, cols] * cp0_ref[:, cols] + cw_ref[1:2, cols] * cp1_ref[:, cols]
                + cw_ref[2:3, cols] * u)

    acc = jnp.zeros(x1.shape, F32)
    for j in range(D_FF // FF_CHUNK):
        gate = conv(slice(j * FF_CHUNK, (j + 1) * FF_CHUNK))
        val = conv(slice(D_FF + j * FF_CHUNK, D_FF + (j + 1) * FF_CHUNK))
        act = (gate * jax.nn.sigmoid(gate) * val).astype(BF16)
        acc = acc + _dot(act, wdn_ref[j * FF_CHUNK:(j + 1) * FF_CHUNK, :])
    y_ref[...] = _rms(x1 + gtf_ref[...] * acc, gfin_ref[...])


def _ffn_sample(x2d, o_r, o_f, mod, cp0, cp1, fp):
    m = x2d.shape[0]
    full = lambda n: pl.BlockSpec((m, n), lambda i: (0, 0))
    mod_spec = lambda j: pl.BlockSpec((m, D_MODEL), lambda i: (0, j))
    const = lambda shape: pl.BlockSpec(shape, lambda i: (0, 0))
    return pl.pallas_call(
        _ffn_sample_kernel, grid=(1,),
        in_specs=[full(D_MODEL), full(D_GROUP), full(D_GROUP),
                  mod_spec(2), mod_spec(3), mod_spec(4), mod_spec(5),
                  const((D_MODEL, D_MODEL)), const((1, D_MODEL)), const((D_MODEL, 2 * D_FF)),
                  const((CONV_W, 2 * D_FF)), const((1, 2 * D_FF)), const((D_FF, D_MODEL)),
                  const((1, D_MODEL)), full(2 * D_FF), full(2 * D_FF)],
        out_specs=[full(D_MODEL), full(2 * D_FF)],
        out_shape=[jax.ShapeDtypeStruct((m, D_MODEL), F32), jax.ShapeDtypeStruct((m, 2 * D_FF), F32)],
        compiler_params=_params(), name="ffn_sample")(
            x2d, o_r, o_f, mod, mod, mod, mod, fp["w_out"], fp["g_ffn"], fp["w_up"], fp["conv_w"],
            fp["conv_b"], fp["w_down"], fp["g_final"], cp0, cp1)


def _rwkv_prep_kernel(z_ref, sh_ref, mu_ref, w0_ref, wup_ref, a0_ref, aup_ref, gup_ref, kk_ref, ka_ref,
                      r_ref, k_ref, v_ref, d_ref, kko_ref, a_ref, g_ref):
    z = z_ref[...]
    zm = z + mu_ref[...] * (sh_ref[...] - z)
    r, k, v, logd, kk, a, g = _rwkv_mix(zm, w0_ref[...], wup_ref[...], a0_ref[...], aup_ref[...],
                                        gup_ref[...], kk_ref[...], ka_ref[...])
    r_ref[...] = r
    k_ref[...] = k
    v_ref[...] = v
    d_ref[...] = jnp.exp(logd)
    kko_ref[...] = kk
    a_ref[...] = a
    g_ref[...] = g


def _rwkv_prep(zr, shift, rp):
    m = zr.shape[0]
    return pl.pallas_call(
        _rwkv_prep_kernel,
        out_shape=[jax.ShapeDtypeStruct((m, D_GROUP), F32)] * 7,
        compiler_params=_params(), name="rwkv_prep")(
            zr, shift, rp["mu"], rp["w0"], rp["wup"], rp["a0"], rp["aup"], rp["gup"], rp["k_k"], rp["k_a"])


def _rwkv_step_kernel(r_ref, k_ref, v_ref, d_ref, kk_ref, a_ref, g_ref, rk_ref, lnw_ref, lnb_ref, s_ref,
                      o_ref, so_ref):
    n = HEAD_DIM
    ii = lax.broadcasted_iota(jnp.int32, (n, n), 0)
    jj = lax.broadcasted_iota(jnp.int32, (n, n), 1)
    eye = ii == jj
    rows = []
    for h in range(N_HEADS):
        hs = slice(h, h + 1)
        r, k, v, d, kk, a = (ref[0, hs, :] for ref in (r_ref, k_ref, v_ref, d_ref, kk_ref, a_ref))
        nrm = jnp.sqrt(jnp.sum(kk * kk, axis=-1, keepdims=True))
        kkn = kk / jnp.maximum(nrm, 1e-12)
        s = s_ref[h]
        sa = -jnp.sum(s * kkn, axis=-1, keepdims=True)
        vcol = jnp.sum(jnp.where(eye, v, 0.0), axis=-1, keepdims=True)
        s = s * d + sa * (kkn * a) + vcol * k
        so_ref[h] = s
        ycol = jnp.sum(s * r, axis=-1, keepdims=True)
        y = jnp.sum(jnp.where(eye, ycol, 0.0), axis=0, keepdims=True)
        bonus = jnp.sum(r * k * rk_ref[hs, :], axis=-1, keepdims=True) * v
        rows.append((_group_norm(y) * lnw_ref[hs, :] + lnb_ref[hs, :] + bonus) * g_ref[0, hs, :])
    o_ref[0] = jnp.concatenate(rows, axis=0)


def _rwkv_step(vecs, g3, rp, state):
    nb = state.shape[0] // N_HEADS
    vec_spec = pl.BlockSpec((1, N_HEADS, HEAD_DIM), lambda b: (b, 0, 0))
    par_spec = pl.BlockSpec((N_HEADS, HEAD_DIM), lambda b: (0, 0))
    st_spec = pl.BlockSpec((N_HEADS, HEAD_DIM, HEAD_DIM), lambda b: (b, 0, 0))
    return pl.pallas_call(
        _rwkv_step_kernel, grid=(nb,),
        in_specs=[vec_spec] * 7 + [par_spec] * 3 + [st_spec],
        out_specs=[vec_spec, st_spec],
        out_shape=[jax.ShapeDtypeStruct((nb, N_HEADS, HEAD_DIM), F32),
                   jax.ShapeDtypeStruct(state.shape, F32)],
        compiler_params=_params(dimension_semantics=("parallel",)), name="rwkv_step")(
            *vecs, g3, rp["r_k2"], rp["ln_w2"], rp["ln_b2"], state)


def _lane_prefix(x):
    lane = lax.broadcasted_iota(jnp.int32, x.shape, 1)
    s = 1
    while s < x.shape[1]:
        x = x + jnp.where(lane >= s, pltpu.roll(x, s, axis=1), 0.0)
        s *= 2
    return x


def _pattn_kernel(npg, pt_ref, q_ref, kn_ref, vn_ref, lfn_ref, *refs):
    del pt_ref
    pg = PAGES_PER_STEP
    k_refs, v_refs, lf_refs = refs[0:pg], refs[pg:2 * pg], refs[2 * pg:3 * pg]
    o_ref, m_s, l_s, c_s, qc_s, acc_s = refs[3 * pg:]
    g = pl.program_id(1)
    n = HEAD_DIM
    eye = lax.broadcasted_iota(jnp.int32, (n, n), 0) == lax.broadcasted_iota(jnp.int32, (n, n), 1)

    @pl.when(g == 0)
    def _():
        m_s[...] = jnp.full_like(m_s, -jnp.inf)
        l_s[...] = jnp.zeros_like(l_s)
        c_s[...] = jnp.zeros_like(c_s)
        acc_s[...] = jnp.zeros_like(acc_s)
        q = q_ref[0].astype(F32)
        for h in range(N_HEADS):
            col = jnp.sum(jnp.where(eye, q[h:h + 1, :], 0.0), axis=-1, keepdims=True)
            qc_s[h] = jnp.broadcast_to(col, (n, PAGE))

    lf = jnp.concatenate([lf_refs[p][0] for p in range(pg)], axis=0)
    within = _lane_prefix(lf)
    run = c_s[:, 0:1]
    scores = []
    for p in range(pg):
        w_p = within[p * N_HEADS:(p + 1) * N_HEADS, :]
        rows = [jnp.sum(qc_s[h] * k_refs[p][0, h], axis=0, keepdims=True) for h in range(N_HEADS)]
        scores.append(jnp.concatenate(rows, axis=0) - (run + w_p))
        run = run + w_p[:, PAGE - 1:PAGE]
    c_s[...] = jnp.broadcast_to(run, c_s.shape)
    m_old = m_s[:, 0:1]
    m_new = m_old
    for s in scores:
        m_new = jnp.maximum(m_new, jnp.max(s, axis=-1, keepdims=True))
    alpha = jnp.exp(m_old - m_new)
    probs = [jnp.exp(s - m_new) for s in scores]
    l = alpha * l_s[:, 0:1]
    for pr in probs:
        l = l + jnp.sum(pr, axis=-1, keepdims=True)
    m_s[...] = jnp.broadcast_to(m_new, m_s.shape)
    l_s[...] = jnp.broadcast_to(l, l_s.shape)
    for h in range(N_HEADS):
        a = acc_s[h] * alpha[h:h + 1, :]
        for p in range(pg):
            a = a + probs[p][h:h + 1, :] * v_refs[p][0, h]
        acc_s[h] = a

    @pl.when(g == npg - 1)
    def _():
        rows = []
        for h in range(N_HEADS):
            col = jnp.sum(acc_s[h], axis=-1, keepdims=True)
            rows.append(jnp.sum(jnp.where(eye, col, 0.0), axis=0, keepdims=True))
        o_past = jnp.concatenate(rows, axis=0)
        r8 = lax.broadcasted_iota(jnp.int32, (N_HEADS, N_HEADS), 0)
        c8 = lax.broadcasted_iota(jnp.int32, (N_HEADS, N_HEADS), 1)
        lfn_col = jnp.sum(jnp.where(r8 == c8, lfn_ref[0], 0.0), axis=-1, keepdims=True)
        s_new = jnp.sum(q_ref[0].astype(F32) * kn_ref[0], axis=-1, keepdims=True) - (run + lfn_col)
        m_fin = jnp.maximum(m_new, s_new)
        beta = jnp.exp(m_new - m_fin)
        p_new = jnp.exp(s_new - m_fin)
        o_ref[0] = (o_past * beta + p_new * vn_ref[0]) / (l * beta + p_new)


def _attn_sample(qb3, kn3, vn3, lfn3, kc, vc, lfc, pt_t):
    n_pages, nb = pt_t.shape
    pg = PAGES_PER_STEP
    npg = n_pages // pg
    vec_spec = pl.BlockSpec((1, N_HEADS, HEAD_DIM), lambda b, g, pt: (b, 0, 0))
    page_spec = lambda p: pl.BlockSpec((1, N_HEADS, HEAD_DIM, PAGE), lambda b, g, pt: (pt[g * pg + p, b], 0, 0, 0))
    lf_spec = lambda p: pl.BlockSpec((1, N_HEADS, PAGE), lambda b, g, pt: (pt[g * pg + p, b], 0, 0))
    grid_spec = pltpu.PrefetchScalarGridSpec(
        num_scalar_prefetch=1, grid=(nb, npg),
        in_specs=[vec_spec, vec_spec, vec_spec, pl.BlockSpec((1, 1, N_HEADS), lambda b, g, pt: (b, 0, 0))]
        + [page_spec(p) for p in range(pg)] + [page_spec(p) for p in range(pg)]
        + [lf_spec(p) for p in range(pg)],
        out_specs=vec_spec,
        scratch_shapes=[pltpu.VMEM((N_HEADS, LANES), F32), pltpu.VMEM((N_HEADS, LANES), F32),
                        pltpu.VMEM((N_HEADS, LANES), F32),
                        pltpu.VMEM((N_HEADS, HEAD_DIM, PAGE), F32), pltpu.VMEM((N_HEADS, HEAD_DIM, PAGE), F32)])
    return pl.pallas_call(
        functools.partial(_pattn_kernel, npg), grid_spec=grid_spec,
        out_shape=jax.ShapeDtypeStruct((nb, N_HEADS, HEAD_DIM), F32),
        compiler_params=_params(dimension_semantics=("parallel", "arbitrary")), name="fox_sample")(
            pt_t, qb3, kn3, vn3, lfn3, *([kc] * pg), *([vc] * pg), *([lfc] * pg))


def _pad_cols(a, n):
    return jnp.pad(a, ((0, 0), (0, n - a.shape[1])))


def kernel(x_prompt, x_sample, c_prompt, c_sample, cache_k, cache_v, cache_logf, page_table, state_wkv,
           state_shift, state_ffn_conv, w_ada, b_ada, g_attn_norm, w_in, b_forget, rwkv_mu, rwkv_w0,
           rwkv_w_up, rwkv_a0, rwkv_a_up, rwkv_g_up, rwkv_k_k, rwkv_k_a, rwkv_r_k, rwkv_ln_w, rwkv_ln_b,
           w_out, g_ffn_norm, w_ffn_up, ffn_conv_w, ffn_conv_b, w_ffn_down, g_final_norm):
    depth = w_in.shape[0]
    assert depth == 1, "one layer per call"
    nb, t_len, _ = x_prompt.shape
    db, dt, _ = x_sample.shape
    assert dt == 1, "the sample group decodes one token per sequence"
    assert t_len % ATTN_TILE == 0 and db % 8 == 0

    w_ada_bf = w_ada[0].astype(BF16)
    b_ada2 = b_ada[0][None, :]
    wi = w_in[0]
    o = D_RWKV_IN
    w_in_bf = jnp.concatenate(
        [_pad_cols(wi[:, :o], ZR_PAD), wi[:, o:o + 3 * D_GROUP], _pad_cols(wi[:, o + 3 * D_GROUP:], F_PAD)],
        axis=1).astype(BF16)
    bfp = _pad_cols(b_forget[0][None, :], F_PAD)
    g_attn = g_attn_norm[0][None, :]
    lora_rows = lambda w, off: jnp.pad(w, ((off, LORA_W - off - w.shape[0]), (0, 0))).astype(BF16)
    row = lambda a: a[None, :]
    rp = dict(mu=_pad_cols(row(rwkv_mu[0]), ZR_PAD), w0=row(rwkv_w0[0]), a0=row(rwkv_a0[0]),
              wup=lora_rows(rwkv_w_up[0], 0), aup=lora_rows(rwkv_a_up[0], W_LORA),
              gup=lora_rows(rwkv_g_up[0], W_LORA + A_LORA),
              k_k=row(rwkv_k_k[0]), k_a=row(rwkv_k_a[0]), ln_w=row(rwkv_ln_w[0]), ln_b=row(rwkv_ln_b[0]),
              r_k3=rwkv_r_k[0][:, None, :], r_k2=rwkv_r_k[0],
              ln_w2=rwkv_ln_w[0].reshape(N_HEADS, HEAD_DIM), ln_b2=rwkv_ln_b[0].reshape(N_HEADS, HEAD_DIM))
    fp = dict(w_out=w_out[0].astype(BF16), g_ffn=row(g_ffn_norm[0]), w_up=w_ffn_up[0].astype(BF16),
              conv_w=ffn_conv_w[0], conv_b=row(ffn_conv_b[0]), w_down=w_ffn_down[0].astype(BF16),
              g_final=row(g_final_norm))

    rows = nb * t_len
    xp = x_prompt.reshape(rows, D_MODEL)
    mod_p = _mod(jnp.pad(c_prompt, ((0, 8 - nb), (0, 0))), w_ada_bf, b_ada2).reshape(8, 1, 6 * D_MODEL)
    zr, qb, k_p, v_p, kb, vb, lf_p, lft = _inproj(xp, mod_p, False, t_len // ROW_TILE, g_attn, w_in_bf, bfp,
                                                  ROW_TILE)
    cum = _cumsum(lft, nb, t_len)
    o_r, st = _rwkv_prompt(zr, jnp.zeros((nb, 1, ZR_PAD), F32), nb, t_len, rp)
    o_f = _attn_prompt(qb, kb, vb, cum, nb, t_len)
    y_p, conv_p = _ffn_prompt(xp, o_r, o_f, mod_p, jnp.zeros((nb, CONV_W - 1, 2 * D_FF), F32), fp, nb, t_len)

    y_prompt = y_p.reshape(nb, t_len, D_MODEL)
    k_prompt = k_p.reshape(1, nb, t_len, N_HEADS, HEAD_DIM)
    v_prompt = v_p.reshape(1, nb, t_len, N_HEADS, HEAD_DIM)
    logf_prompt = lf_p.reshape(1, nb, t_len, N_HEADS)
    wkv_prompt = jnp.swapaxes(st, -1, -2).reshape(1, nb, N_HEADS, HEAD_DIM, HEAD_DIM)
    shift_prompt = zr.reshape(nb, t_len, ZR_PAD)[:, -1, :D_RWKV_IN][None]
    conv_prompt = conv_p[None]

    xs = x_sample.reshape(db, D_MODEL)
    mod_s = _mod(c_sample, w_ada_bf, b_ada2)
    zr_s, qb_s, k_s, v_s, _, _, lf_s, _ = _inproj(xs, mod_s, True, 1, g_attn, w_in_bf, bfp, db)
    vecs = _rwkv_prep(zr_s, _pad_cols(state_shift[0], ZR_PAD), rp)
    to3 = lambda a: a.reshape(db, N_HEADS, HEAD_DIM)
    o_r_s, wkv_s = _rwkv_step([to3(a) for a in vecs[:6]], to3(vecs[6]), rp,
                              state_wkv[0].reshape(db * N_HEADS, HEAD_DIM, HEAD_DIM))
    kc = jnp.transpose(cache_k[0], (0, 2, 3, 1))
    vc = jnp.transpose(cache_v[0], (0, 2, 3, 1))
    lfc = jnp.transpose(cache_logf[0], (0, 2, 1))
    o_f_s = _attn_sample(to3(qb_s), to3(k_s), to3(v_s), lf_s.reshape(db, 1, N_HEADS), kc, vc, lfc,
                         page_table.T)
    cp = state_ffn_conv[0]
    y_s, u_s = _ffn_sample(xs, o_r_s.reshape(db, D_GROUP).astype(BF16), o_f_s.reshape(db, D_GROUP).astype(BF16),
                           mod_s, cp[:, 0, :], cp[:, 1, :], fp)

    y_sample = y_s.reshape(db, 1, D_MODEL)
    k_sample = k_s.reshape(1, db, 1, N_HEADS, HEAD_DIM)
    v_sample = v_s.reshape(1, db, 1, N_HEADS, HEAD_DIM)
    logf_sample = lf_s.reshape(1, db, 1, N_HEADS)
    wkv_sample = wkv_s.reshape(1, db, N_HEADS, HEAD_DIM, HEAD_DIM)
    shift_sample = zr_s[:, :D_RWKV_IN][None]
    conv_sample = jnp.stack([cp[:, 1, :], u_s], axis=1)[None]
    return (y_prompt, y_sample, k_prompt, v_prompt, logf_prompt, wkv_prompt, shift_prompt, conv_prompt,
            k_sample, v_sample, logf_sample, wkv_sample, shift_sample, conv_sample)
```

```python
import functools

import jax
import jax.numpy as jnp
from jax import lax
from jax.experimental import pallas as pl
from jax.experimental.pallas import tpu as pltpu

F32 = jnp.float32
BF16 = jnp.bfloat16

D_MODEL = 1024
HEAD_DIM = 64
N_HEADS = 8
D_GROUP = N_HEADS * HEAD_DIM
W_LORA, A_LORA, G_LORA = 32, 32, 96
D_RWKV_IN = 3 * D_GROUP + W_LORA + A_LORA + G_LORA
ZR_PAD = 1792
LORA_OFF = 3 * D_GROUP
LORA_W = ZR_PAD - LORA_OFF
F_PAD = 128
W_IN_PAD = ZR_PAD + 3 * D_GROUP + F_PAD
D_FF = 2816
CONV_W = 3
PAGE = 128
RMS_EPS = 1e-6
GN_EPS = 64e-5
ATTN_SCALE = HEAD_DIM ** -0.5
LANES = 128
VMEM_LIMIT = 56 * 1024 * 1024

ROW_TILE = 256
RWKV_TILE = 256
CHUNK = 64
ATTN_TILE = 512
FF_CHUNK = 256
PAGES_PER_STEP = 16


def _params(**kw):
    return pltpu.CompilerParams(vmem_limit_bytes=VMEM_LIMIT, **kw)


def _dot(a, b):
    return jnp.dot(a, b, preferred_element_type=F32)


def _dot_nt(a, b):
    return lax.dot_general(a, b, (((1,), (1,)), ((), ())), preferred_element_type=F32)


def _dot_tn(a, b):
    return lax.dot_general(a, b, (((0,), (0,)), ((), ())), preferred_element_type=F32)


def _bdot(a, b):
    return lax.dot_general(a, b, (((2,), (1,)), ((0,), (0,))), preferred_element_type=F32)


def _bdot_nt(a, b):
    return lax.dot_general(a, b, (((2,), (2,)), ((0,), (0,))), preferred_element_type=F32)


def _bdot_tn(a, b):
    return lax.dot_general(a, b, (((1,), (1,)), ((0,), (0,))), preferred_element_type=F32)


def _split3(x):
    hi = x.astype(BF16)
    r1 = x - hi.astype(F32)
    mid = r1.astype(BF16)
    lo = (r1 - mid.astype(F32)).astype(BF16)
    return hi, mid, lo


def _rms(x, g):
    return x * lax.rsqrt(jnp.mean(x * x, axis=-1, keepdims=True) + RMS_EPS) * g


def _mod_kernel(c_ref, w_ref, b_ref, o_ref):
    c = c_ref[...]
    s = (c * jax.nn.sigmoid(c)).astype(BF16)
    o_ref[...] = _dot(s, w_ref[...]) + b_ref[...]


def _mod(c, w_bf, b):
    m, n, tn = c.shape[0], w_bf.shape[1], 1536
    return pl.pallas_call(
        _mod_kernel, grid=(n // tn,),
        in_specs=[pl.BlockSpec((m, D_MODEL), lambda j: (0, 0)),
                  pl.BlockSpec((D_MODEL, tn), lambda j: (0, j)),
                  pl.BlockSpec((1, tn), lambda j: (0, j))],
        out_specs=pl.BlockSpec((m, tn), lambda j: (0, j)),
        out_shape=jax.ShapeDtypeStruct((m, n), F32),
        compiler_params=_params(), name="adaln_mod")(c, w_bf, b)


def _project(per_row, x_ref, sh_ref, sc_ref, g_ref, w_ref, bf_ref):
    sh = sh_ref[...] if per_row else sh_ref[0]
    sc = sc_ref[...] if per_row else sc_ref[0]
    h = (_rms(x_ref[...], g_ref[...]) * (1 + sc) + sh).astype(BF16)
    o = ZR_PAD
    zr = _dot(h, w_ref[:, 0:o])
    q = _dot(h, w_ref[:, o:o + D_GROUP]) * ATTN_SCALE
    k = _dot(h, w_ref[:, o + D_GROUP:o + 2 * D_GROUP])
    v = _dot(h, w_ref[:, o + 2 * D_GROUP:o + 3 * D_GROUP])
    f = _dot(h, w_ref[:, o + 3 * D_GROUP:o + 3 * D_GROUP + F_PAD])
    return zr, q, k, v, jax.nn.log_sigmoid(f + bf_ref[...])


def _inproj_prompt_kernel(x_ref, sh_ref, sc_ref, g_ref, w_ref, bf_ref,
                          zr_ref, qb_ref, kb_ref, vb_ref, kt_ref, vt_ref, lft_ref, k_s, v_s):
    zr, q, k, v, lf = _project(False, x_ref, sh_ref, sc_ref, g_ref, w_ref, bf_ref)
    zr_ref[...] = zr
    qb_ref[...] = q.astype(BF16)
    kb_ref[...] = k.astype(BF16)
    vb_ref[...] = v.astype(BF16)
    k_s[...] = k
    v_s[...] = v
    kt_ref[0] = k_s[...].T
    vt_ref[0] = v_s[...].T
    lft_ref[0] = lf.T[0:N_HEADS, :]


def _inproj_sample_kernel(x_ref, sh_ref, sc_ref, g_ref, w_ref, bf_ref,
                          zrt_ref, qb_ref, k_ref, v_ref, kt_ref, vt_ref, lft_ref, zr_s):
    zr, q, k, v, lf = _project(True, x_ref, sh_ref, sc_ref, g_ref, w_ref, bf_ref)
    qb_ref[...] = q.astype(BF16)
    k_ref[...] = k
    v_ref[...] = v
    zr_s[...] = zr
    zrt_ref[...] = zr_s[...].T
    kt_ref[...] = k_ref[...].T
    vt_ref[...] = v_ref[...].T
    lft_ref[...] = lf.T[0:N_HEADS, :]


def _inproj_prompt(x2d, mod3, nb, t_len, g, w_bf, bfp):
    tm = ROW_TILE
    tpb = t_len // tm
    rows = nb * t_len
    mod_spec = lambda j: pl.BlockSpec((1, 1, D_MODEL), lambda i: (i // tpb, 0, j))
    row_spec = lambda n: pl.BlockSpec((tm, n), lambda i: (i, 0))
    col_spec = lambda n: pl.BlockSpec((1, n, tm), lambda i: (i // tpb, 0, i % tpb))
    const = lambda shape: pl.BlockSpec(shape, lambda i: (0, 0))
    return pl.pallas_call(
        _inproj_prompt_kernel, grid=(rows // tm,),
        in_specs=[row_spec(D_MODEL), mod_spec(0), mod_spec(1), const((1, D_MODEL)),
                  const((D_MODEL, W_IN_PAD)), const((1, F_PAD))],
        out_specs=[row_spec(ZR_PAD), row_spec(D_GROUP), row_spec(D_GROUP), row_spec(D_GROUP),
                   col_spec(D_GROUP), col_spec(D_GROUP), col_spec(N_HEADS)],
        out_shape=[jax.ShapeDtypeStruct((rows, ZR_PAD), F32),
                   jax.ShapeDtypeStruct((rows, D_GROUP), BF16),
                   jax.ShapeDtypeStruct((rows, D_GROUP), BF16),
                   jax.ShapeDtypeStruct((rows, D_GROUP), BF16),
                   jax.ShapeDtypeStruct((nb, D_GROUP, t_len), F32),
                   jax.ShapeDtypeStruct((nb, D_GROUP, t_len), F32),
                   jax.ShapeDtypeStruct((nb, N_HEADS, t_len), F32)],
        scratch_shapes=[pltpu.VMEM((tm, D_GROUP), F32), pltpu.VMEM((tm, D_GROUP), F32)],
        compiler_params=_params(dimension_semantics=("parallel",)), name="in_proj")(
            x2d, mod3, mod3, g, w_bf, bfp)


def _inproj_sample(x2d, mod, g, w_bf, bfp):
    m = x2d.shape[0]
    full = lambda r, c: pl.BlockSpec((r, c), lambda i: (0, 0))
    mod_spec = lambda j: pl.BlockSpec((m, D_MODEL), lambda i: (0, j))
    return pl.pallas_call(
        _inproj_sample_kernel, grid=(1,),
        in_specs=[full(m, D_MODEL), mod_spec(0), mod_spec(1), full(1, D_MODEL), full(D_MODEL, W_IN_PAD),
                  full(1, F_PAD)],
        out_specs=[full(ZR_PAD, m), full(m, D_GROUP), full(m, D_GROUP), full(m, D_GROUP),
                   full(D_GROUP, m), full(D_GROUP, m), full(N_HEADS, m)],
        out_shape=[jax.ShapeDtypeStruct((ZR_PAD, m), F32),
                   jax.ShapeDtypeStruct((m, D_GROUP), BF16),
                   jax.ShapeDtypeStruct((m, D_GROUP), F32),
                   jax.ShapeDtypeStruct((m, D_GROUP), F32),
                   jax.ShapeDtypeStruct((D_GROUP, m), F32),
                   jax.ShapeDtypeStruct((D_GROUP, m), F32),
                   jax.ShapeDtypeStruct((N_HEADS, m), F32)],
        scratch_shapes=[pltpu.VMEM((m, ZR_PAD), F32)],
        compiler_params=_params(), name="in_proj_sample")(x2d, mod, mod, g, w_bf, bfp)


def _cumsum_kernel(x_ref, o_ref):
    nb, _, t_len = x_ref.shape
    blk = 256
    r = lax.broadcasted_iota(jnp.int32, (blk, blk), 0)
    c = lax.broadcasted_iota(jnp.int32, (blk, blk), 1)
    upper = (r <= c).astype(BF16)
    for b in range(nb):
        carry = jnp.zeros((N_HEADS, 1), F32)
        for j in range(t_len // blk):
            sl = slice(j * blk, (j + 1) * blk)
            hi, mid, lo = _split3(x_ref[b, :, sl])
            cs = (_dot(lo, upper) + _dot(mid, upper)) + _dot(hi, upper) + carry
            o_ref[b, :, sl] = cs
            carry = cs[:, blk - 1:blk]


def _cumsum(lft):
    return pl.pallas_call(
        _cumsum_kernel, out_shape=jax.ShapeDtypeStruct(lft.shape, F32),
        compiler_params=_params(), name="logf_cumsum")(lft)


def _rwkv_mix(zm, w0, wup, a0, aup, gup, k_k, k_a):
    r = zm[:, 0:D_GROUP]
    k = zm[:, D_GROUP:2 * D_GROUP]
    v = zm[:, 2 * D_GROUP:3 * D_GROUP]
    lora = zm[:, LORA_OFF:ZR_PAD]
    w = w0 + _dot(jnp.tanh(lora).astype(BF16), wup)
    logd = -jnp.exp(-jax.nn.softplus(-w) - 0.5)
    a = jax.nn.sigmoid(a0 + _dot(lora.astype(BF16), aup))
    g = _dot(jax.nn.sigmoid(lora).astype(BF16), gup)
    kk = k * k_k
    k = k * (1 + (a - 1) * k_a)
    return r, k, v, logd, kk, a, g


def _group_norm(y):
    mean = jnp.mean(y, axis=-1, keepdims=True)
    var = jnp.mean(jnp.square(y - mean), axis=-1, keepdims=True)
    return (y - mean) * lax.rsqrt(var + GN_EPS)


def _rwkv_kernel(z_ref, sh0_ref, mu_ref, w0_ref, wup_ref, a0_ref, aup_ref, gup_ref, kk_ref, ka_ref,
                 rk_ref, lnw_ref, lnb_ref, o_ref, st_ref, prev_s, st_s, hm_s, yn_s, bon_s):
    tt, c = RWKV_TILE, CHUNK
    t = pl.program_id(1)

    @pl.when(t == 0)
    def _():
        prev_s[0:1, :] = sh0_ref[0]
        st_s[...] = jnp.zeros_like(st_s)

    z = z_ref[...]
    row = lax.broadcasted_iota(jnp.int32, (tt, 1), 0)
    zprev = jnp.where(row == 0, prev_s[0:1, :], pltpu.roll(z, 1, axis=0))
    prev_s[0:1, :] = z[tt - 1:tt, :]
    zm = z + mu_ref[...] * (zprev - z)
    r, k, v, logd, kk, a, g = _rwkv_mix(zm, w0_ref[...], wup_ref[...], a0_ref[...], aup_ref[...],
                                        gup_ref[...], kk_ref[...], ka_ref[...])
    ri = lax.broadcasted_iota(jnp.int32, (tt, tt), 0)
    ci = lax.broadcasted_iota(jnp.int32, (tt, tt), 1)
    tri = ((ri // c == ci // c) & (ci <= ri)).astype(BF16)
    hi, mid, lo = _split3(logd)
    cl = (_dot(tri, lo) + _dot(tri, mid)) + _dot(tri, hi)
    for h in range(N_HEADS):
        sl = slice(h * HEAD_DIM, (h + 1) * HEAD_DIM)
        for n, val in enumerate((r, k, v, kk, a, logd, cl)):
            hm_s[n, h] = val[:, sl]

    rr = lax.broadcasted_iota(jnp.int32, (c, c), 0)
    cc = lax.broadcasted_iota(jnp.int32, (c, c), 1)
    strict = (cc < rr).astype(F32)
    eye = rr == cc
    r2 = lax.broadcasted_iota(jnp.int32, (c, 2 * c), 0)
    c2 = lax.broadcasted_iota(jnp.int32, (c, 2 * c), 1)
    sgn = jnp.where(c2 < c, jnp.where(c2 <= r2, 1.0, 0.0), jnp.where(c2 - c <= r2, -1.0, 0.0)).astype(F32)
    nch = tt // c
    grp = N_HEADS * nch
    r_, k_, v_, kk_, a_, ld_, cl_ = (hm_s[n].reshape(grp, c, HEAD_DIM) for n in range(7))
    nrm = jnp.sqrt(jnp.sum(kk_ * kk_, axis=-1, keepdims=True))
    kkn = kk_ / jnp.maximum(nrm, 1e-12)
    b_ = kkn * a_
    cl_end = cl_[:, c - 1:c, :]
    e_end = jnp.exp(cl_end - cl_)
    g_inv = jnp.exp(-cl_)
    kkg = kkn * jnp.exp(cl_ - ld_)
    rg = r_ * jnp.exp(cl_)
    kd = (k_ * g_inv).astype(BF16)
    bd = (b_ * g_inv).astype(BF16)
    kkg_b = kkg.astype(BF16)
    a_k = _bdot_nt(kkg_b, kd) * strict
    a_b = _bdot_nt(kkg_b, bd) * strict
    ll = _bdot_nt(rg.astype(BF16), jnp.concatenate([kd, bd], axis=1)) * sgn
    akv = _bdot(a_k.astype(BF16), v_.astype(BF16))
    x = jnp.concatenate([akv, kkg], axis=2)
    p = (-a_b).astype(BF16)
    x = x + _bdot(p, x.astype(BF16))
    n = 2
    while n < c:
        p = _bdot(p, p).astype(BF16)
        x = x + _bdot(p, x.astype(BF16))
        n *= 2
    rhs = jnp.concatenate([jnp.concatenate([v_, jnp.zeros_like(v_)], axis=2), x], axis=1).astype(BF16)
    yq = _bdot(ll.astype(BF16), rhs)
    kb = jnp.concatenate([k_ * e_end, -(b_ * e_end)], axis=1).astype(BF16)
    sw = _bdot_tn(kb, rhs)
    qc = (rg + yq[:, :, HEAD_DIM:]).astype(BF16)
    m = (jnp.where(eye, jnp.broadcast_to(jnp.exp(cl_end), (grp, c, c)), 0.0) + sw[:, :, HEAD_DIM:]).astype(BF16)
    by_head = lambda t: t.reshape(N_HEADS, nch, *t.shape[1:])
    qc, m, y0, w = by_head(qc), by_head(m), by_head(yq[:, :, 0:HEAD_DIM]), by_head(sw[:, :, 0:HEAD_DIM])
    st = st_s[...]
    ys = []
    for i in range(nch):
        st_b = st.astype(BF16)
        ys.append(_bdot(qc[:, i], st_b) + y0[:, i])
        st = _bdot(m[:, i], st_b) + w[:, i]
    st_s[...] = st
    yn_s[...] = _group_norm(jnp.concatenate(ys, axis=1))
    bon_s[...] = jnp.sum(hm_s[0] * hm_s[1] * rk_ref[...], axis=-1, keepdims=True) * hm_s[2]
    yn = jnp.concatenate([yn_s[h] for h in range(N_HEADS)], axis=1)
    bon = jnp.concatenate([bon_s[h] for h in range(N_HEADS)], axis=1)
    o_ref[...] = ((yn * lnw_ref[...] + lnb_ref[...] + bon) * g).astype(BF16)
    st_ref[...] = st_s[...]


def _rwkv_prompt(zr, shift0, nb, t_len, rp):
    tt = RWKV_TILE
    nt = t_len // tt
    const = lambda shape: pl.BlockSpec(shape, lambda b, t: (0,) * len(shape))
    return pl.pallas_call(
        _rwkv_kernel, grid=(nb, nt),
        in_specs=[pl.BlockSpec((tt, ZR_PAD), lambda b, t: (b * nt + t, 0)),
                  pl.BlockSpec((1, 1, ZR_PAD), lambda b, t: (b, 0, 0)),
                  const((1, ZR_PAD)), const((1, D_GROUP)), const((LORA_W, D_GROUP)),
                  const((1, D_GROUP)), const((LORA_W, D_GROUP)), const((LORA_W, D_GROUP)),
                  const((1, D_GROUP)), const((1, D_GROUP)), const((N_HEADS, 1, HEAD_DIM)),
                  const((1, D_GROUP)), const((1, D_GROUP))],
        out_specs=[pl.BlockSpec((tt, D_GROUP), lambda b, t: (b * nt + t, 0)),
                   pl.BlockSpec((N_HEADS, HEAD_DIM, HEAD_DIM), lambda b, t: (b, 0, 0))],
        out_shape=[jax.ShapeDtypeStruct((nb * t_len, D_GROUP), BF16),
                   jax.ShapeDtypeStruct((nb * N_HEADS, HEAD_DIM, HEAD_DIM), F32)],
        scratch_shapes=[pltpu.VMEM((8, ZR_PAD), F32),
                        pltpu.VMEM((N_HEADS, HEAD_DIM, HEAD_DIM), F32),
                        pltpu.VMEM((7, N_HEADS, tt, HEAD_DIM), F32),
                        pltpu.VMEM((N_HEADS, tt, HEAD_DIM), F32),
                        pltpu.VMEM((N_HEADS, tt, HEAD_DIM), F32)],
        compiler_params=_params(dimension_semantics=("arbitrary", "arbitrary")), name="rwkv_scan")(
            zr, shift0, rp["mu"], rp["w0"], rp["wup"], rp["a0"], rp["aup"], rp["gup"], rp["k_k"],
            rp["k_a"], rp["r_k3"], rp["ln_w"], rp["ln_b"])


def _attn_kernel(q_ref, k_ref, v_ref, c_ref, o_ref):
    tq = ATTN_TILE
    qi = pl.program_id(2)
    q2 = q_ref[...]
    lane = lax.broadcasted_iota(jnp.int32, (1, LANES), 1)
    lo = lane < HEAD_DIM
    zero = jnp.zeros_like(q2)
    q_h = (jnp.where(lo, q2, zero), jnp.where(lo, zero, q2))
    c_ref0 = c_ref[0, 0, :, pl.ds(pl.multiple_of(qi * tq, tq), LANES)][:, 0:1]
    row = lax.broadcasted_iota(jnp.int32, (tq, tq), 0)
    col = lax.broadcasted_iota(jnp.int32, (tq, tq), 1)
    causal = col <= row

    def step(j, carry, masked):
        off = pl.multiple_of(j * tq, tq)
        kb = k_ref[pl.ds(off, tq), :]
        vb = v_ref[pl.ds(off, tq), :]
        ck = c_ref[0, 0, :, pl.ds(off, tq)]
        out = []
        for hd in range(2):
            m, l, acc = carry[3 * hd:3 * hd + 3]
            s = _dot_nt(q_h[hd], kb) + (c_ref0[hd:hd + 1, :] - ck[hd:hd + 1, :])
            if masked:
                s = jnp.where(causal, s, -jnp.inf)
            m_new = jnp.maximum(m, jnp.max(s, axis=-1, keepdims=True))
            alpha = jnp.exp(m - m_new)
            p = jnp.exp(s - m_new)
            l = alpha * l + jnp.sum(p, axis=-1, keepdims=True)
            acc = alpha * acc + _dot(p.astype(BF16), vb)
            out += [m_new, l, acc]
        return tuple(out)

    init = (jnp.full((tq, 1), -jnp.inf, F32), jnp.zeros((tq, 1), F32), jnp.zeros((tq, LANES), F32)) * 2
    carry = lax.fori_loop(0, qi, lambda j, cr: step(j, cr, False), init)
    _, l0, acc0, _, l1, acc1 = step(qi, carry, True)
    o_ref[...] = jnp.where(lo, acc0 / l0, acc1 / l1).astype(BF16)


def _attn_prompt(qb, kb, vb, cum, nb, t_len):
    tq = ATTN_TILE
    nq = t_len // tq
    n_pairs = D_GROUP // LANES
    cum4 = cum.reshape(nb, n_pairs, 2, t_len)
    return pl.pallas_call(
        _attn_kernel, grid=(nb, n_pairs, nq),
        in_specs=[pl.BlockSpec((tq, LANES), lambda b, hp, qi: (b * nq + qi, hp)),
                  pl.BlockSpec((t_len, LANES), lambda b, hp, qi: (b, hp)),
                  pl.BlockSpec((t_len, LANES), lambda b, hp, qi: (b, hp)),
                  pl.BlockSpec((1, 1, 2, t_len), lambda b, hp, qi: (b, hp, 0, 0))],
        out_specs=pl.BlockSpec((tq, LANES), lambda b, hp, qi: (b * nq + qi, hp)),
        out_shape=jax.ShapeDtypeStruct((nb * t_len, D_GROUP), BF16),
        compiler_params=_params(dimension_semantics=("parallel", "parallel", "arbitrary")),
        name="fox_prompt")(qb, kb, vb, cum4)


def _ffn_front(x, o_r, o_f, gta, shf, scf, wout_ref, gffn):
    attn = _dot(o_r, wout_ref[0:D_GROUP, :]) + _dot(o_f, wout_ref[D_GROUP:2 * D_GROUP, :])
    x1 = x + gta * attn
    return x1, (_rms(x1, gffn) * (1 + scf) + shf).astype(BF16)


def _ffn_prompt_kernel(tpb, x_ref, or_ref, of_ref, gta_ref, shf_ref, scf_ref, gtf_ref, wout_ref, gffn_ref,
                       wup_ref, cw_ref, cb_ref, wdn_ref, gfin_ref, cp_ref, y_ref, conv_ref, prev_s):
    i = pl.program_id(0)

    @pl.when(i % tpb == 0)
    def _():
        prev_s[0:2, :] = cp_ref[0]

    x1, h2 = _ffn_front(x_ref[...], or_ref[...], of_ref[...], gta_ref[0], shf_ref[0], scf_ref[0],
                        wout_ref, gffn_ref[...])
    tm = x1.shape[0]
    row = lax.broadcasted_iota(jnp.int32, (tm, 1), 0)

    def conv(cols):
        u = _dot(h2, wup_ref[:, cols])
        p0 = prev_s[0:1, cols]
        p1 = prev_s[1:2, cols]
        u1 = jnp.where(row == 0, p1, pltpu.roll(u, 1, axis=0))
        u2 = jnp.where(row == 0, p0, jnp.where(row == 1, p1, pltpu.roll(u, 2, axis=0)))
        prev_s[0:2, cols] = u[tm - 2:tm, :]
        return cb_ref[:, cols] + cw_ref[0:1, cols] * u2 + cw_ref[1:2, cols] * u1 + cw_ref[2:3, cols] * u

    acc = jnp.zeros((tm, D_MODEL), F32)
    for j in range(D_FF // FF_CHUNK):
        gate = conv(slice(j * FF_CHUNK, (j + 1) * FF_CHUNK))
        val = conv(slice(D_FF + j * FF_CHUNK, D_FF + (j + 1) * FF_CHUNK))
        act = (gate * jax.nn.sigmoid(gate) * val).astype(BF16)
        acc = acc + _dot(act, wdn_ref[j * FF_CHUNK:(j + 1) * FF_CHUNK, :])
    y_ref[...] = _rms(x1 + gtf_ref[0] * acc, gfin_ref[...])
    conv_ref[0] = prev_s[0:2, :]


def _ffn_prompt(x2d, o_r, o_f, mod3, conv0, fp, nb, t_len):
    tm = ROW_TILE
    tpb = t_len // tm
    rows = nb * t_len
    row_spec = lambda n: pl.BlockSpec((tm, n), lambda i: (i, 0))
    mod_spec = lambda j: pl.BlockSpec((1, 1, D_MODEL), lambda i: (i // tpb, 0, j))
    const = lambda shape: pl.BlockSpec(shape, lambda i: (0, 0))
    conv_spec = pl.BlockSpec((1, CONV_W - 1, 2 * D_FF), lambda i: (i // tpb, 0, 0))
    return pl.pallas_call(
        functools.partial(_ffn_prompt_kernel, tpb), grid=(rows // tm,),
        in_specs=[row_spec(D_MODEL), row_spec(D_GROUP), row_spec(D_GROUP),
                  mod_spec(2), mod_spec(3), mod_spec(4), mod_spec(5),
                  const((D_MODEL, D_MODEL)), const((1, D_MODEL)), const((D_MODEL, 2 * D_FF)),
                  const((CONV_W, 2 * D_FF)), const((1, 2 * D_FF)), const((D_FF, D_MODEL)),
                  const((1, D_MODEL)), conv_spec],
        out_specs=[row_spec(D_MODEL), conv_spec],
        out_shape=[jax.ShapeDtypeStruct((rows, D_MODEL), F32),
                   jax.ShapeDtypeStruct((nb, CONV_W - 1, 2 * D_FF), F32)],
        scratch_shapes=[pltpu.VMEM((8, 2 * D_FF), F32)],
        compiler_params=_params(dimension_semantics=("arbitrary",)), name="ffn_prompt")(
            x2d, o_r, o_f, mod3, mod3, mod3, mod3, fp["w_out"], fp["g_ffn"], fp["w_up"], fp["conv_w"],
            fp["conv_b"], fp["w_down"], fp["g_final"], conv0)


def _ffn_sample_kernel(x_ref, ort_ref, of_ref, gta_ref, shf_ref, scf_ref, gtf_ref, wout_ref, gffn_ref,
                       wup_ref, cw_ref, cb_ref, wdn_ref, gfin_ref, cp0_ref, cp1_ref, y_ref, u_ref):
    attn = (_dot(ort_ref[...].T.astype(BF16), wout_ref[0:D_GROUP, :])
            + _dot(of_ref[...].astype(BF16), wout_ref[D_GROUP:2 * D_GROUP, :]))
    x1 = x_ref[...] + gta_ref[...] * attn
    h2 = (_rms(x1, gffn_ref[...]) * (1 + scf_ref[...]) + shf_ref[...]).astype(BF16)

    def conv(cols):
        u = _dot(h2, wup_ref[:, cols])
        u_ref[:, cols] = u
        return (cb_ref[:, cols] + cw_ref[0:1, cols] * cp0_ref[:, cols] + cw_ref[1:2, cols] * cp1_ref[:, cols]
                + cw_ref[2:3, cols] * u)

    acc = jnp.zeros(x1.shape, F32)
    for j in range(D_FF // FF_CHUNK):
        gate = conv(slice(j * FF_CHUNK, (j + 1) * FF_CHUNK))
        val = conv(slice(D_FF + j * FF_CHUNK, D_FF + (j + 1) * FF_CHUNK))
        act = (gate * jax.nn.sigmoid(gate) * val).astype(BF16)
        acc = acc + _dot(act, wdn_ref[j * FF_CHUNK:(j + 1) * FF_CHUNK, :])
    y_ref[...] = _rms(x1 + gtf_ref[...] * acc, gfin_ref[...])


def _ffn_sample(x2d, o_r_t, o_f, mod, cp0, cp1, fp):
    m = x2d.shape[0]
    full = lambda n: pl.BlockSpec((m, n), lambda i: (0, 0))
    mod_spec = lambda j: pl.BlockSpec((m, D_MODEL), lambda i: (0, j))
    const = lambda shape: pl.BlockSpec(shape, lambda i: (0, 0))
    return pl.pallas_call(
        _ffn_sample_kernel, grid=(1,),
        in_specs=[full(D_MODEL), const((D_GROUP, m)), full(D_GROUP),
                  mod_spec(2), mod_spec(3), mod_spec(4), mod_spec(5),
                  const((D_MODEL, D_MODEL)), const((1, D_MODEL)), const((D_MODEL, 2 * D_FF)),
                  const((CONV_W, 2 * D_FF)), const((1, 2 * D_FF)), const((D_FF, D_MODEL)),
                  const((1, D_MODEL)), full(2 * D_FF), full(2 * D_FF)],
        out_specs=[full(D_MODEL), full(2 * D_FF)],
        out_shape=[jax.ShapeDtypeStruct((m, D_MODEL), F32), jax.ShapeDtypeStruct((m, 2 * D_FF), F32)],
        compiler_params=_params(), name="ffn_sample")(
            x2d, o_r_t, o_f, mod, mod, mod, mod, fp["w_out"], fp["g_ffn"], fp["w_up"], fp["conv_w"],
            fp["conv_b"], fp["w_down"], fp["g_final"], cp0, cp1)


def _rwkv_prep_kernel(zt_ref, sht_ref, mu_ref, w0_ref, wupt_ref, a0_ref, aupt_ref, gupt_ref, kk_ref, ka_ref,
                      r_ref, k_ref, v_ref, d_ref, kko_ref, a_ref, g_ref):
    z = zt_ref[...]
    zm = z + mu_ref[...] * (sht_ref[...] - z)
    k = zm[D_GROUP:2 * D_GROUP, :]
    lora = zm[LORA_OFF:ZR_PAD, :]
    w = w0_ref[...] + _dot(wupt_ref[...], jnp.tanh(lora).astype(BF16))
    a = jax.nn.sigmoid(a0_ref[...] + _dot(aupt_ref[...], lora.astype(BF16)))
    r_ref[...] = zm[0:D_GROUP, :]
    k_ref[...] = k * (1 + (a - 1) * ka_ref[...])
    v_ref[...] = zm[2 * D_GROUP:3 * D_GROUP, :]
    d_ref[...] = jnp.exp(-jnp.exp(-jax.nn.softplus(-w) - 0.5))
    kko_ref[...] = k * kk_ref[...]
    a_ref[...] = a
    g_ref[...] = _dot(gupt_ref[...], jax.nn.sigmoid(lora).astype(BF16))


def _rwkv_prep(zrt, shift_t, rp):
    m = zrt.shape[1]
    return pl.pallas_call(
        _rwkv_prep_kernel,
        out_shape=[jax.ShapeDtypeStruct((D_GROUP, m), F32)] * 7,
        compiler_params=_params(), name="rwkv_prep")(
            zrt, shift_t, rp["mu_c"], rp["w0_c"], rp["wup_t"], rp["a0_c"], rp["aup_t"], rp["gup_t"],
            rp["k_k_c"], rp["k_a_c"])


def _rwkv_step_kernel(r_ref, k_ref, v_ref, d_ref, kk_ref, a_ref, g_ref, rk_ref, lnw_ref, lnb_ref, s_ref,
                      o_ref, so_ref, y_s):
    r, k, d, kk, a = (ref[0] for ref in (r_ref, k_ref, d_ref, kk_ref, a_ref))
    nrm = jnp.sqrt(jnp.sum(kk * kk, axis=0, keepdims=True))
    kkn = kk / jnp.maximum(nrm, 1e-12)
    bb = kkn * a

    def body(i, carry):
        s = s_ref[0, i]
        sa = -jnp.sum(s * kkn, axis=0, keepdims=True)
        s = s * d + sa * bb + v_ref[0, pl.ds(i, 1), :] * k
        so_ref[0, i] = s
        y_s[pl.ds(i, 1), :] = jnp.sum(s * r, axis=0, keepdims=True)
        return carry

    lax.fori_loop(0, HEAD_DIM, body, 0)
    y = y_s[...]
    mean = jnp.mean(y, axis=0, keepdims=True)
    var = jnp.mean(jnp.square(y - mean), axis=0, keepdims=True)
    yn = (y - mean) * lax.rsqrt(var + GN_EPS)
    bonus = jnp.sum(r * k * rk_ref[0], axis=0, keepdims=True) * v_ref[0]
    o_ref[0] = (yn * lnw_ref[0] + lnb_ref[0] + bonus) * g_ref[0]


def _rwkv_step(vecs, rp, state_t):
    m = state_t.shape[-1]
    vec_spec = pl.BlockSpec((1, HEAD_DIM, m), lambda h: (h, 0, 0))
    par_spec = pl.BlockSpec((1, HEAD_DIM, 1), lambda h: (h, 0, 0))
    st_spec = pl.BlockSpec((1, HEAD_DIM, HEAD_DIM, m), lambda h: (h, 0, 0, 0))
    return pl.pallas_call(
        _rwkv_step_kernel, grid=(N_HEADS,),
        in_specs=[vec_spec] * 7 + [par_spec] * 3 + [st_spec],
        out_specs=[vec_spec, st_spec],
        out_shape=[jax.ShapeDtypeStruct((N_HEADS, HEAD_DIM, m), F32),
                   jax.ShapeDtypeStruct(state_t.shape, F32)],
        scratch_shapes=[pltpu.VMEM((HEAD_DIM, m), F32)],
        compiler_params=_params(dimension_semantics=("parallel",)), name="rwkv_step")(
            *vecs, rp["r_k_c"], rp["ln_w_c"], rp["ln_b_c"], state_t)


def _lane_prefix(x):
    lane = lax.broadcasted_iota(jnp.int32, x.shape, 1)
    s = 1
    while s < x.shape[1]:
        x = x + jnp.where(lane >= s, pltpu.roll(x, s, axis=1), 0.0)
        s *= 2
    return x


def _pattn_kernel(npg, pt_ref, q_ref, kn_ref, vn_ref, lfn_ref, *refs):
    pg = PAGES_PER_STEP
    kc_ref, vc_ref, lfc_ref, o_ref, kbuf, vbuf, lfbuf, sem, m_s, l_s, c_s, qc_s, acc_s = refs
    g = pl.program_id(1)
    n_seq = pl.num_programs(0)
    step = pl.program_id(0) * npg + g
    last = n_seq * npg - 1
    slot = step % 2

    def page_copies(st, sl, fetch):
        seq, grp = st // npg, st % npg
        out = []
        for p in range(pg):
            page = pt_ref[grp * pg + p, seq] if fetch else 0
            out.append(pltpu.make_async_copy(kc_ref.at[page], kbuf.at[sl, p], sem.at[sl, 0]))
            out.append(pltpu.make_async_copy(vc_ref.at[page], vbuf.at[sl, p], sem.at[sl, 1]))
            out.append(pltpu.make_async_copy(lfc_ref.at[page], lfbuf.at[sl, p], sem.at[sl, 2]))
        return out

    @pl.when(step == 0)
    def _():
        for cp in page_copies(step, slot, True):
            cp.start()

    for cp in page_copies(step, slot, False):
        cp.wait()
    nxt = jnp.minimum(step + 1, last)
    for cp in page_copies(nxt, 1 - slot, True):
        cp.start()

    n = HEAD_DIM
    eye = lax.broadcasted_iota(jnp.int32, (n, n), 0) == lax.broadcasted_iota(jnp.int32, (n, n), 1)

    @pl.when(g == 0)
    def _():
        m_s[...] = jnp.full_like(m_s, -jnp.inf)
        l_s[...] = jnp.zeros_like(l_s)
        c_s[...] = jnp.zeros_like(c_s)
        acc_s[...] = jnp.zeros_like(acc_s)
        q = q_ref[0].astype(F32)
        for h in range(N_HEADS):
            col = jnp.sum(jnp.where(eye, q[h:h + 1, :], 0.0), axis=-1, keepdims=True)
            qc_s[h] = jnp.broadcast_to(col, (n, PAGE))

    within = _lane_prefix(lfbuf[slot].reshape(pg * N_HEADS, PAGE))
    run = c_s[:, 0:1]
    scores = []
    for p in range(pg):
        w_p = within[p * N_HEADS:(p + 1) * N_HEADS, :]
        rows = [jnp.sum(qc_s[h] * kbuf[slot, p, h], axis=0, keepdims=True) for h in range(N_HEADS)]
        scores.append(jnp.concatenate(rows, axis=0) - (run + w_p))
        run = run + w_p[:, PAGE - 1:PAGE]
    c_s[...] = jnp.broadcast_to(run, c_s.shape)
    m_old = m_s[:, 0:1]
    m_new = m_old
    for s in scores:
        m_new = jnp.maximum(m_new, jnp.max(s, axis=-1, keepdims=True))
    alpha = jnp.exp(m_old - m_new)
    probs = [jnp.exp(s - m_new) for s in scores]
    l = alpha * l_s[:, 0:1]
    for pr in probs:
        l = l + jnp.sum(pr, axis=-1, keepdims=True)
    m_s[...] = jnp.broadcast_to(m_new, m_s.shape)
    l_s[...] = jnp.broadcast_to(l, l_s.shape)
    for h in range(N_HEADS):
        a = acc_s[h] * alpha[h:h + 1, :]
        for p in range(pg):
            a = a + probs[p][h:h + 1, :] * vbuf[slot, p, h]
        acc_s[h] = a

    @pl.when(step == last)
    def _():
        for cp in page_copies(step, 1 - slot, False):
            cp.wait()

    @pl.when(g == npg - 1)
    def _():
        rows = []
        for h in range(N_HEADS):
            col = jnp.sum(acc_s[h], axis=-1, keepdims=True)
            rows.append(jnp.sum(jnp.where(eye, col, 0.0), axis=0, keepdims=True))
        o_past = jnp.concatenate(rows, axis=0)
        r8 = lax.broadcasted_iota(jnp.int32, (N_HEADS, N_HEADS), 0)
        c8 = lax.broadcasted_iota(jnp.int32, (N_HEADS, N_HEADS), 1)
        lfn_col = jnp.sum(jnp.where(r8 == c8, lfn_ref[0], 0.0), axis=-1, keepdims=True)
        s_new = jnp.sum(q_ref[0].astype(F32) * kn_ref[0], axis=-1, keepdims=True) - (run + lfn_col)
        m_fin = jnp.maximum(m_new, s_new)
        beta = jnp.exp(m_new - m_fin)
        p_new = jnp.exp(s_new - m_fin)
        o_ref[0] = (o_past * beta + p_new * vn_ref[0]) / (l * beta + p_new)


def _attn_sample(qb3, kn3, vn3, lfn3, kc, vc, lfc, pt_t):
    n_pages, nb = pt_t.shape
    pg = PAGES_PER_STEP
    npg = n_pages // pg
    vec_spec = pl.BlockSpec((1, N_HEADS, HEAD_DIM), lambda b, g, pt: (b, 0, 0))
    hbm = pl.BlockSpec(memory_space=pl.ANY)
    grid_spec = pltpu.PrefetchScalarGridSpec(
        num_scalar_prefetch=1, grid=(nb, npg),
        in_specs=[vec_spec, vec_spec, vec_spec, pl.BlockSpec((1, 1, N_HEADS), lambda b, g, pt: (b, 0, 0)),
                  hbm, hbm, hbm],
        out_specs=vec_spec,
        scratch_shapes=[pltpu.VMEM((2, pg, N_HEADS, HEAD_DIM, PAGE), F32),
                        pltpu.VMEM((2, pg, N_HEADS, HEAD_DIM, PAGE), F32),
                        pltpu.VMEM((2, pg, N_HEADS, PAGE), F32),
                        pltpu.SemaphoreType.DMA((2, 3)),
                        pltpu.VMEM((N_HEADS, LANES), F32), pltpu.VMEM((N_HEADS, LANES), F32),
                        pltpu.VMEM((N_HEADS, LANES), F32),
                        pltpu.VMEM((N_HEADS, HEAD_DIM, PAGE), F32), pltpu.VMEM((N_HEADS, HEAD_DIM, PAGE), F32)])
    return pl.pallas_call(
        functools.partial(_pattn_kernel, npg), grid_spec=grid_spec,
        out_shape=jax.ShapeDtypeStruct((nb, N_HEADS, HEAD_DIM), F32),
        compiler_params=_params(dimension_semantics=("arbitrary", "arbitrary")), name="fox_sample")(
            pt_t, qb3, kn3, vn3, lfn3, kc, vc, lfc)


def _pad_cols(a, n):
    return jnp.pad(a, ((0, 0), (0, n - a.shape[1])))


def kernel(x_prompt, x_sample, c_prompt, c_sample, cache_k, cache_v, cache_logf, page_table, state_wkv,
           state_shift, state_ffn_conv, w_ada, b_ada, g_attn_norm, w_in, b_forget, rwkv_mu, rwkv_w0,
           rwkv_w_up, rwkv_a0, rwkv_a_up, rwkv_g_up, rwkv_k_k, rwkv_k_a, rwkv_r_k, rwkv_ln_w, rwkv_ln_b,
           w_out, g_ffn_norm, w_ffn_up, ffn_conv_w, ffn_conv_b, w_ffn_down, g_final_norm):
    depth = w_in.shape[0]
    assert depth == 1, "one layer per call"
    nb, t_len, _ = x_prompt.shape
    db, dt, _ = x_sample.shape
    assert dt == 1, "the sample group decodes one token per sequence"
    assert t_len % ATTN_TILE == 0 and db % 8 == 0

    w_ada_bf = w_ada[0].astype(BF16)
    b_ada2 = b_ada[0][None, :]
    wi = w_in[0]
    o = D_RWKV_IN
    w_in_bf = jnp.concatenate(
        [_pad_cols(wi[:, :o], ZR_PAD), wi[:, o:o + 3 * D_GROUP], _pad_cols(wi[:, o + 3 * D_GROUP:], F_PAD)],
        axis=1).astype(BF16)
    bfp = _pad_cols(b_forget[0][None, :], F_PAD)
    g_attn = g_attn_norm[0][None, :]
    lora_rows = lambda w, off: jnp.pad(w, ((off, LORA_W - off - w.shape[0]), (0, 0))).astype(BF16)
    row = lambda a: a[None, :]
    col = lambda a: a[:, None]
    head_col = lambda a: a.reshape(N_HEADS, HEAD_DIM, 1)
    rp = dict(mu=_pad_cols(row(rwkv_mu[0]), ZR_PAD), w0=row(rwkv_w0[0]), a0=row(rwkv_a0[0]),
              wup=lora_rows(rwkv_w_up[0], 0), aup=lora_rows(rwkv_a_up[0], W_LORA),
              gup=lora_rows(rwkv_g_up[0], W_LORA + A_LORA),
              k_k=row(rwkv_k_k[0]), k_a=row(rwkv_k_a[0]), ln_w=row(rwkv_ln_w[0]), ln_b=row(rwkv_ln_b[0]),
              r_k3=rwkv_r_k[0][:, None, :])
    rp.update(mu_c=rp["mu"].T, w0_c=col(rwkv_w0[0]), a0_c=col(rwkv_a0[0]), k_k_c=col(rwkv_k_k[0]),
              k_a_c=col(rwkv_k_a[0]), wup_t=rp["wup"].T, aup_t=rp["aup"].T, gup_t=rp["gup"].T,
              r_k_c=head_col(rwkv_r_k[0]), ln_w_c=head_col(rwkv_ln_w[0]), ln_b_c=head_col(rwkv_ln_b[0]))
    fp = dict(w_out=w_out[0].astype(BF16), g_ffn=row(g_ffn_norm[0]), w_up=w_ffn_up[0].astype(BF16),
              conv_w=ffn_conv_w[0], conv_b=row(ffn_conv_b[0]), w_down=w_ffn_down[0].astype(BF16),
              g_final=row(g_final_norm))

    rows = nb * t_len
    xp = x_prompt.reshape(rows, D_MODEL)
    mod_p = _mod(jnp.pad(c_prompt, ((0, 8 - nb), (0, 0))), w_ada_bf, b_ada2).reshape(8, 1, 6 * D_MODEL)
    zr, qb, kb, vb, kt_p, vt_p, lft = _inproj_prompt(xp, mod_p, nb, t_len, g_attn, w_in_bf, bfp)
    cum = _cumsum(lft)
    o_r, st = _rwkv_prompt(zr, jnp.zeros((nb, 1, ZR_PAD), F32), nb, t_len, rp)
    o_f = _attn_prompt(qb, kb, vb, cum, nb, t_len)
    y_p, conv_p = _ffn_prompt(xp, o_r, o_f, mod_p, jnp.zeros((nb, CONV_W - 1, 2 * D_FF), F32), fp, nb, t_len)

    heads_t = lambda a: jnp.transpose(a.reshape(nb, N_HEADS, HEAD_DIM, t_len), (0, 3, 1, 2))[None]
    y_prompt = y_p.reshape(nb, t_len, D_MODEL)
    k_prompt = heads_t(kt_p)
    v_prompt = heads_t(vt_p)
    logf_prompt = jnp.transpose(lft, (0, 2, 1))[None]
    wkv_prompt = jnp.swapaxes(st, -1, -2).reshape(1, nb, N_HEADS, HEAD_DIM, HEAD_DIM)
    shift_prompt = zr.reshape(nb, t_len, ZR_PAD)[:, -1, :D_RWKV_IN][None]
    conv_prompt = conv_p[None]

    xs = x_sample.reshape(db, D_MODEL)
    mod_s = _mod(c_sample, w_ada_bf, b_ada2)
    zrt_s, qb_s, k_s, v_s, kt_s, vt_s, lft_s = _inproj_sample(xs, mod_s, g_attn, w_in_bf, bfp)
    shift_t = jnp.pad(state_shift[0].T, ((0, ZR_PAD - D_RWKV_IN), (0, 0)))
    vecs = _rwkv_prep(zrt_s, shift_t, rp)
    hd3 = lambda a: a.reshape(N_HEADS, HEAD_DIM, db)
    o_r_t, wkv_t = _rwkv_step([hd3(a) for a in vecs], rp, jnp.transpose(state_wkv[0], (1, 2, 3, 0)))
    kc = jnp.transpose(cache_k[0], (0, 2, 3, 1))
    vc = jnp.transpose(cache_v[0], (0, 2, 3, 1))
    lfc = jnp.transpose(cache_logf[0], (0, 2, 1))
    to3 = lambda a: a.reshape(db, N_HEADS, HEAD_DIM)
    o_f_s = _attn_sample(to3(qb_s), to3(k_s), to3(v_s), lft_s.T.reshape(db, 1, N_HEADS), kc, vc, lfc,
                         page_table.T)
    cp = state_ffn_conv[0]
    y_s, u_s = _ffn_sample(xs, o_r_t.reshape(D_GROUP, db), o_f_s.reshape(db, D_GROUP), mod_s,
                           cp[:, 0, :], cp[:, 1, :], fp)

    heads_s = lambda a: jnp.transpose(hd3(a), (2, 0, 1)).reshape(1, db, 1, N_HEADS, HEAD_DIM)
    y_sample = y_s.reshape(db, 1, D_MODEL)
    k_sample = heads_s(kt_s)
    v_sample = heads_s(vt_s)
    logf_sample = lft_s.T.reshape(1, db, 1, N_HEADS)
    wkv_sample = jnp.transpose(wkv_t, (3, 0, 1, 2))[None]
    shift_sample = zrt_s[:D_RWKV_IN].T[None]
    conv_sample = jnp.stack([cp[:, 1, :], u_s], axis=1)[None]
    return (y_prompt, y_sample, k_prompt, v_prompt, logf_prompt, wkv_prompt, shift_prompt, conv_prompt,
            k_sample, v_sample, logf_sample, wkv_sample, shift_sample, conv_sample)
```

```python
import functools

import jax
import jax.numpy as jnp
from jax import lax
from jax.experimental import pallas as pl
from jax.experimental.pallas import tpu as pltpu

F32 = jnp.float32
BF16 = jnp.bfloat16

D_MODEL = 1024
HEAD_DIM = 64
N_HEADS = 8
D_GROUP = N_HEADS * HEAD_DIM
W_LORA, A_LORA, G_LORA = 32, 32, 96
D_RWKV_IN = 3 * D_GROUP + W_LORA + A_LORA + G_LORA
ZR_PAD = 1792
LORA_OFF = 3 * D_GROUP
LORA_W = ZR_PAD - LORA_OFF
F_PAD = 128
W_IN_PAD = ZR_PAD + 3 * D_GROUP + F_PAD
D_FF = 2816
CONV_W = 3
PAGE = 128
RMS_EPS = 1e-6
GN_EPS = 64e-5
ATTN_SCALE = HEAD_DIM ** -0.5
LANES = 128
VMEM_LIMIT = 56 * 1024 * 1024

ROW_TILE = 256
RWKV_TILE = 256
CHUNK = 64
ATTN_TILE = 512
FF_CHUNK = 256
PAGES_PER_STEP = 16
PAGE_SLOTS = 3
FFN_TILE = 512


def _params(**kw):
    return pltpu.CompilerParams(vmem_limit_bytes=VMEM_LIMIT, **kw)


def _dot(a, b):
    return jnp.dot(a, b, preferred_element_type=F32)


def _dot_nt(a, b):
    return lax.dot_general(a, b, (((1,), (1,)), ((), ())), preferred_element_type=F32)


def _dot_tn(a, b):
    return lax.dot_general(a, b, (((0,), (0,)), ((), ())), preferred_element_type=F32)


def _bdot(a, b):
    return lax.dot_general(a, b, (((2,), (1,)), ((0,), (0,))), preferred_element_type=F32)


def _bdot_nt(a, b):
    return lax.dot_general(a, b, (((2,), (2,)), ((0,), (0,))), preferred_element_type=F32)


def _bdot_tn(a, b):
    return lax.dot_general(a, b, (((1,), (1,)), ((0,), (0,))), preferred_element_type=F32)


def _split3(x):
    hi = x.astype(BF16)
    r1 = x - hi.astype(F32)
    mid = r1.astype(BF16)
    lo = (r1 - mid.astype(F32)).astype(BF16)
    return hi, mid, lo


def _rms(x, g):
    return x * lax.rsqrt(jnp.mean(x * x, axis=-1, keepdims=True) + RMS_EPS) * g


def _mod_kernel(c_ref, w_ref, b_ref, o_ref):
    c = c_ref[...]
    s = (c * jax.nn.sigmoid(c)).astype(BF16)
    o_ref[...] = _dot(s, w_ref[...]) + b_ref[...]


def _mod(c, w_bf, b):
    m, n, tn = c.shape[0], w_bf.shape[1], 1536
    return pl.pallas_call(
        _mod_kernel, grid=(n // tn,),
        in_specs=[pl.BlockSpec((m, D_MODEL), lambda j: (0, 0)),
                  pl.BlockSpec((D_MODEL, tn), lambda j: (0, j)),
                  pl.BlockSpec((1, tn), lambda j: (0, j))],
        out_specs=pl.BlockSpec((m, tn), lambda j: (0, j)),
        out_shape=jax.ShapeDtypeStruct((m, n), F32),
        compiler_params=_params(), name="adaln_mod")(c, w_bf, b)


def _project(per_row, x_ref, sh_ref, sc_ref, g_ref, w_ref, bf_ref):
    sh = sh_ref[...] if per_row else sh_ref[0]
    sc = sc_ref[...] if per_row else sc_ref[0]
    h = (_rms(x_ref[...], g_ref[...]) * (1 + sc) + sh).astype(BF16)
    o = ZR_PAD
    zr = _dot(h, w_ref[:, 0:o])
    q = _dot(h, w_ref[:, o:o + D_GROUP]) * ATTN_SCALE
    k = _dot(h, w_ref[:, o + D_GROUP:o + 2 * D_GROUP])
    v = _dot(h, w_ref[:, o + 2 * D_GROUP:o + 3 * D_GROUP])
    f = _dot(h, w_ref[:, o + 3 * D_GROUP:o + 3 * D_GROUP + F_PAD])
    return zr, q, k, v, jax.nn.log_sigmoid(f + bf_ref[...])


def _inproj_prompt_kernel(x_ref, sh_ref, sc_ref, g_ref, w_ref, bf_ref,
                          zr_ref, qb_ref, kb_ref, vb_ref, kt_ref, vt_ref, lft_ref, k_s, v_s):
    zr, q, k, v, lf = _project(False, x_ref, sh_ref, sc_ref, g_ref, w_ref, bf_ref)
    zr_ref[...] = zr
    qb_ref[...] = q.astype(BF16)
    kb_ref[...] = k.astype(BF16)
    vb_ref[...] = v.astype(BF16)
    k_s[...] = k
    v_s[...] = v
    kt_ref[0] = k_s[...].T
    vt_ref[0] = v_s[...].T
    lft_ref[0] = lf.T[0:N_HEADS, :]


def _inproj_sample_kernel(x_ref, sh_ref, sc_ref, g_ref, w_ref, bf_ref,
                          zrt_ref, qb_ref, k_ref, v_ref, kt_ref, vt_ref, lft_ref, zr_s):
    zr, q, k, v, lf = _project(True, x_ref, sh_ref, sc_ref, g_ref, w_ref, bf_ref)
    qb_ref[...] = q.astype(BF16)
    k_ref[...] = k
    v_ref[...] = v
    zr_s[...] = zr
    zrt_ref[...] = zr_s[...].T
    kt_ref[...] = k_ref[...].T
    vt_ref[...] = v_ref[...].T
    lft_ref[...] = lf.T[0:N_HEADS, :]


def _inproj_prompt(x2d, mod3, nb, t_len, g, w_bf, bfp):
    tm = ROW_TILE
    tpb = t_len // tm
    rows = nb * t_len
    mod_spec = lambda j: pl.BlockSpec((1, 1, D_MODEL), lambda i: (i // tpb, 0, j))
    row_spec = lambda n: pl.BlockSpec((tm, n), lambda i: (i, 0))
    col_spec = lambda n: pl.BlockSpec((1, n, tm), lambda i: (i // tpb, 0, i % tpb))
    const = lambda shape: pl.BlockSpec(shape, lambda i: (0, 0))
    return pl.pallas_call(
        _inproj_prompt_kernel, grid=(rows // tm,),
        in_specs=[row_spec(D_MODEL), mod_spec(0), mod_spec(1), const((1, D_MODEL)),
                  const((D_MODEL, W_IN_PAD)), const((1, F_PAD))],
        out_specs=[row_spec(ZR_PAD), row_spec(D_GROUP), row_spec(D_GROUP), row_spec(D_GROUP),
                   col_spec(D_GROUP), col_spec(D_GROUP), col_spec(N_HEADS)],
        out_shape=[jax.ShapeDtypeStruct((rows, ZR_PAD), F32),
                   jax.ShapeDtypeStruct((rows, D_GROUP), BF16),
                   jax.ShapeDtypeStruct((rows, D_GROUP), BF16),
                   jax.ShapeDtypeStruct((rows, D_GROUP), BF16),
                   jax.ShapeDtypeStruct((nb, D_GROUP, t_len), F32),
                   jax.ShapeDtypeStruct((nb, D_GROUP, t_len), F32),
                   jax.ShapeDtypeStruct((nb, N_HEADS, t_len), F32)],
        scratch_shapes=[pltpu.VMEM((tm, D_GROUP), F32), pltpu.VMEM((tm, D_GROUP), F32)],
        compiler_params=_params(dimension_semantics=("parallel",)), name="in_proj")(
            x2d, mod3, mod3, g, w_bf, bfp)


def _inproj_sample(x2d, mod, g, w_bf, bfp):
    m = x2d.shape[0]
    full = lambda r, c: pl.BlockSpec((r, c), lambda i: (0, 0))
    mod_spec = lambda j: pl.BlockSpec((m, D_MODEL), lambda i: (0, j))
    return pl.pallas_call(
        _inproj_sample_kernel, grid=(1,),
        in_specs=[full(m, D_MODEL), mod_spec(0), mod_spec(1), full(1, D_MODEL), full(D_MODEL, W_IN_PAD),
                  full(1, F_PAD)],
        out_specs=[full(ZR_PAD, m), full(m, D_GROUP), full(m, D_GROUP), full(m, D_GROUP),
                   full(D_GROUP, m), full(D_GROUP, m), full(N_HEADS, m)],
        out_shape=[jax.ShapeDtypeStruct((ZR_PAD, m), F32),
                   jax.ShapeDtypeStruct((m, D_GROUP), BF16),
                   jax.ShapeDtypeStruct((m, D_GROUP), F32),
                   jax.ShapeDtypeStruct((m, D_GROUP), F32),
                   jax.ShapeDtypeStruct((D_GROUP, m), F32),
                   jax.ShapeDtypeStruct((D_GROUP, m), F32),
                   jax.ShapeDtypeStruct((N_HEADS, m), F32)],
        scratch_shapes=[pltpu.VMEM((m, ZR_PAD), F32)],
        compiler_params=_params(), name="in_proj_sample")(x2d, mod, mod, g, w_bf, bfp)


def _cumsum_kernel(x_ref, o_ref):
    nb, _, t_len = x_ref.shape
    blk = 256
    r = lax.broadcasted_iota(jnp.int32, (blk, blk), 0)
    c = lax.broadcasted_iota(jnp.int32, (blk, blk), 1)
    upper = (r <= c).astype(BF16)
    for b in range(nb):
        carry = jnp.zeros((N_HEADS, 1), F32)
        for j in range(t_len // blk):
            sl = slice(j * blk, (j + 1) * blk)
            hi, mid, lo = _split3(x_ref[b, :, sl])
            cs = (_dot(lo, upper) + _dot(mid, upper)) + _dot(hi, upper) + carry
            o_ref[b, :, sl] = cs
            carry = cs[:, blk - 1:blk]


def _cumsum(lft):
    return pl.pallas_call(
        _cumsum_kernel, out_shape=jax.ShapeDtypeStruct(lft.shape, F32),
        compiler_params=_params(), name="logf_cumsum")(lft)


def _rwkv_mix(zm, w0, wup, a0, aup, gup, k_k, k_a):
    r = zm[:, 0:D_GROUP]
    k = zm[:, D_GROUP:2 * D_GROUP]
    v = zm[:, 2 * D_GROUP:3 * D_GROUP]
    lora = zm[:, LORA_OFF:ZR_PAD]
    w = w0 + _dot(jnp.tanh(lora).astype(BF16), wup)
    logd = -jnp.exp(-jax.nn.softplus(-w) - 0.5)
    a = jax.nn.sigmoid(a0 + _dot(lora.astype(BF16), aup))
    g = _dot(jax.nn.sigmoid(lora).astype(BF16), gup)
    kk = k * k_k
    k = k * (1 + (a - 1) * k_a)
    return r, k, v, logd, kk, a, g


def _group_norm(y):
    mean = jnp.mean(y, axis=-1, keepdims=True)
    var = jnp.mean(jnp.square(y - mean), axis=-1, keepdims=True)
    return (y - mean) * lax.rsqrt(var + GN_EPS)


def _rwkv_kernel(z_ref, sh0_ref, mu_ref, w0_ref, wup_ref, a0_ref, aup_ref, gup_ref, kk_ref, ka_ref,
                 rk_ref, lnw_ref, lnb_ref, o_ref, st_ref, prev_s, st_s, hm_s, yn_s, bon_s):
    tt, c = RWKV_TILE, CHUNK
    t = pl.program_id(1)

    @pl.when(t == 0)
    def _():
        prev_s[0:1, :] = sh0_ref[0]
        st_s[...] = jnp.zeros_like(st_s)

    z = z_ref[...]
    row = lax.broadcasted_iota(jnp.int32, (tt, 1), 0)
    zprev = jnp.where(row == 0, prev_s[0:1, :], pltpu.roll(z, 1, axis=0))
    prev_s[0:1, :] = z[tt - 1:tt, :]
    zm = z + mu_ref[...] * (zprev - z)
    r, k, v, logd, kk, a, g = _rwkv_mix(zm, w0_ref[...], wup_ref[...], a0_ref[...], aup_ref[...],
                                        gup_ref[...], kk_ref[...], ka_ref[...])
    ri = lax.broadcasted_iota(jnp.int32, (tt, tt), 0)
    ci = lax.broadcasted_iota(jnp.int32, (tt, tt), 1)
    tri = ((ri // c == ci // c) & (ci <= ri)).astype(BF16)
    hi, mid, lo = _split3(logd)
    cl = (_dot(tri, lo) + _dot(tri, mid)) + _dot(tri, hi)
    for h in range(N_HEADS):
        sl = slice(h * HEAD_DIM, (h + 1) * HEAD_DIM)
        for n, val in enumerate((r, k, v, kk, a, logd, cl)):
            hm_s[n, h] = val[:, sl]

    rr = lax.broadcasted_iota(jnp.int32, (c, c), 0)
    cc = lax.broadcasted_iota(jnp.int32, (c, c), 1)
    strict = (cc < rr).astype(F32)
    eye = rr == cc
    r2 = lax.broadcasted_iota(jnp.int32, (c, 2 * c), 0)
    c2 = lax.broadcasted_iota(jnp.int32, (c, 2 * c), 1)
    sgn = jnp.where(c2 < c, jnp.where(c2 <= r2, 1.0, 0.0), jnp.where(c2 - c <= r2, -1.0, 0.0)).astype(F32)
    nch = tt // c
    grp = N_HEADS * nch
    r_, k_, v_, kk_, a_, ld_, cl_ = (hm_s[n].reshape(grp, c, HEAD_DIM) for n in range(7))
    nrm = jnp.sqrt(jnp.sum(kk_ * kk_, axis=-1, keepdims=True))
    kkn = kk_ / jnp.maximum(nrm, 1e-12)
    b_ = kkn * a_
    cl_end = cl_[:, c - 1:c, :]
    e_end = jnp.exp(cl_end - cl_)
    g_inv = jnp.exp(-cl_)
    kkg = kkn * jnp.exp(cl_ - ld_)
    rg = r_ * jnp.exp(cl_)
    kd = (k_ * g_inv).astype(BF16)
    bd = (b_ * g_inv).astype(BF16)
    kkg_b = kkg.astype(BF16)
    a_k = _bdot_nt(kkg_b, kd) * strict
    a_b = _bdot_nt(kkg_b, bd) * strict
    ll = _bdot_nt(rg.astype(BF16), jnp.concatenate([kd, bd], axis=1)) * sgn
    akv = _bdot(a_k.astype(BF16), v_.astype(BF16))
    x = jnp.concatenate([akv, kkg], axis=2)
    p = (-a_b).astype(BF16)
    x = x + _bdot(p, x.astype(BF16))
    n = 2
    while n < c:
        p = _bdot(p, p).astype(BF16)
        x = x + _bdot(p, x.astype(BF16))
        n *= 2
    rhs = jnp.concatenate([jnp.concatenate([v_, jnp.zeros_like(v_)], axis=2), x], axis=1).astype(BF16)
    yq = _bdot(ll.astype(BF16), rhs)
    kb = jnp.concatenate([k_ * e_end, -(b_ * e_end)], axis=1).astype(BF16)
    sw = _bdot_tn(kb, rhs)
    qc = (rg + yq[:, :, HEAD_DIM:]).astype(BF16)
    m = (jnp.where(eye, jnp.broadcast_to(jnp.exp(cl_end), (grp, c, c)), 0.0) + sw[:, :, HEAD_DIM:]).astype(BF16)
    by_head = lambda t: t.reshape(N_HEADS, nch, *t.shape[1:])
    qc, m, y0, w = by_head(qc), by_head(m), by_head(yq[:, :, 0:HEAD_DIM]), by_head(sw[:, :, 0:HEAD_DIM])
    st = st_s[...]
    ys = []
    for i in range(nch):
        st_b = st.astype(BF16)
        ys.append(_bdot(qc[:, i], st_b) + y0[:, i])
        st = _bdot(m[:, i], st_b) + w[:, i]
    st_s[...] = st
    yn_s[...] = _group_norm(jnp.concatenate(ys, axis=1))
    bon_s[...] = jnp.sum(hm_s[0] * hm_s[1] * rk_ref[...], axis=-1, keepdims=True) * hm_s[2]
    yn = jnp.concatenate([yn_s[h] for h in range(N_HEADS)], axis=1)
    bon = jnp.concatenate([bon_s[h] for h in range(N_HEADS)], axis=1)
    o_ref[...] = ((yn * lnw_ref[...] + lnb_ref[...] + bon) * g).astype(BF16)
    st_ref[...] = st_s[...]


def _rwkv_prompt(zr, shift0, nb, t_len, rp):
    tt = RWKV_TILE
    nt = t_len // tt
    const = lambda shape: pl.BlockSpec(shape, lambda b, t: (0,) * len(shape))
    return pl.pallas_call(
        _rwkv_kernel, grid=(nb, nt),
        in_specs=[pl.BlockSpec((tt, ZR_PAD), lambda b, t: (b * nt + t, 0)),
                  pl.BlockSpec((1, 1, ZR_PAD), lambda b, t: (b, 0, 0)),
                  const((1, ZR_PAD)), const((1, D_GROUP)), const((LORA_W, D_GROUP)),
                  const((1, D_GROUP)), const((LORA_W, D_GROUP)), const((LORA_W, D_GROUP)),
                  const((1, D_GROUP)), const((1, D_GROUP)), const((N_HEADS, 1, HEAD_DIM)),
                  const((1, D_GROUP)), const((1, D_GROUP))],
        out_specs=[pl.BlockSpec((tt, D_GROUP), lambda b, t: (b * nt + t, 0)),
                   pl.BlockSpec((N_HEADS, HEAD_DIM, HEAD_DIM), lambda b, t: (b, 0, 0))],
        out_shape=[jax.ShapeDtypeStruct((nb * t_len, D_GROUP), BF16),
                   jax.ShapeDtypeStruct((nb * N_HEADS, HEAD_DIM, HEAD_DIM), F32)],
        scratch_shapes=[pltpu.VMEM((8, ZR_PAD), F32),
                        pltpu.VMEM((N_HEADS, HEAD_DIM, HEAD_DIM), F32),
                        pltpu.VMEM((7, N_HEADS, tt, HEAD_DIM), F32),
                        pltpu.VMEM((N_HEADS, tt, HEAD_DIM), F32),
                        pltpu.VMEM((N_HEADS, tt, HEAD_DIM), F32)],
        compiler_params=_params(dimension_semantics=("arbitrary", "arbitrary")), name="rwkv_scan")(
            zr, shift0, rp["mu"], rp["w0"], rp["wup"], rp["a0"], rp["aup"], rp["gup"], rp["k_k"],
            rp["k_a"], rp["r_k3"], rp["ln_w"], rp["ln_b"])


def _attn_kernel(q_ref, k_ref, v_ref, c_ref, o_ref):
    tq = ATTN_TILE
    qi = pl.program_id(2)
    q2 = q_ref[...]
    lane = lax.broadcasted_iota(jnp.int32, (1, LANES), 1)
    lo = lane < HEAD_DIM
    zero = jnp.zeros_like(q2)
    q_h = (jnp.where(lo, q2, zero), jnp.where(lo, zero, q2))
    c_ref0 = c_ref[0, 0, :, pl.ds(pl.multiple_of(qi * tq, tq), LANES)][:, 0:1]
    row = lax.broadcasted_iota(jnp.int32, (tq, tq), 0)
    col = lax.broadcasted_iota(jnp.int32, (tq, tq), 1)
    causal = col <= row

    def step(j, carry, masked):
        off = pl.multiple_of(j * tq, tq)
        kb = k_ref[pl.ds(off, tq), :]
        vb = v_ref[pl.ds(off, tq), :]
        ck = c_ref[0, 0, :, pl.ds(off, tq)]
        out = []
        for hd in range(2):
            m, l, acc = carry[3 * hd:3 * hd + 3]
            s = _dot_nt(q_h[hd], kb) + (c_ref0[hd:hd + 1, :] - ck[hd:hd + 1, :])
            if masked:
                s = jnp.where(causal, s, -jnp.inf)
            m_new = jnp.maximum(m, jnp.max(s, axis=-1, keepdims=True))
            alpha = jnp.exp(m - m_new)
            p = jnp.exp(s - m_new)
            l = alpha * l + jnp.sum(p, axis=-1, keepdims=True)
            acc = alpha * acc + _dot(p.astype(BF16), vb)
            out += [m_new, l, acc]
        return tuple(out)

    init = (jnp.full((tq, 1), -jnp.inf, F32), jnp.zeros((tq, 1), F32), jnp.zeros((tq, LANES), F32)) * 2
    carry = lax.fori_loop(0, qi, lambda j, cr: step(j, cr, False), init)
    _, l0, acc0, _, l1, acc1 = step(qi, carry, True)
    o_ref[...] = jnp.where(lo, acc0 / l0, acc1 / l1).astype(BF16)


def _attn_prompt(qb, kb, vb, cum, nb, t_len):
    tq = ATTN_TILE
    nq = t_len // tq
    n_pairs = D_GROUP // LANES
    cum4 = cum.reshape(nb, n_pairs, 2, t_len)
    return pl.pallas_call(
        _attn_kernel, grid=(nb, n_pairs, nq),
        in_specs=[pl.BlockSpec((tq, LANES), lambda b, hp, qi: (b * nq + qi, hp)),
                  pl.BlockSpec((t_len, LANES), lambda b, hp, qi: (b, hp)),
                  pl.BlockSpec((t_len, LANES), lambda b, hp, qi: (b, hp)),
                  pl.BlockSpec((1, 1, 2, t_len), lambda b, hp, qi: (b, hp, 0, 0))],
        out_specs=pl.BlockSpec((tq, LANES), lambda b, hp, qi: (b * nq + qi, hp)),
        out_shape=jax.ShapeDtypeStruct((nb * t_len, D_GROUP), BF16),
        compiler_params=_params(dimension_semantics=("parallel", "parallel", "arbitrary")),
        name="fox_prompt")(qb, kb, vb, cum4)


def _ffn_front(x, o_r, o_f, gta, shf, scf, wout_ref, gffn):
    attn = _dot(o_r, wout_ref[0:D_GROUP, :]) + _dot(o_f, wout_ref[D_GROUP:2 * D_GROUP, :])
    x1 = x + gta * attn
    return x1, (_rms(x1, gffn) * (1 + scf) + shf).astype(BF16)


def _ffn_prompt_kernel(tpb, x_ref, or_ref, of_ref, gta_ref, shf_ref, scf_ref, gtf_ref, wout_ref, gffn_ref,
                       wup_ref, cw_ref, cb_ref, wdn_ref, gfin_ref, cp_ref, y_ref, conv_ref, prev_s):
    i = pl.program_id(0)

    @pl.when(i % tpb == 0)
    def _():
        prev_s[0:2, :] = cp_ref[0]

    x1, h2 = _ffn_front(x_ref[...], or_ref[...], of_ref[...], gta_ref[0], shf_ref[0], scf_ref[0],
                        wout_ref, gffn_ref[...])
    tm = x1.shape[0]
    row = lax.broadcasted_iota(jnp.int32, (tm, 1), 0)

    def conv(cols):
        u = _dot(h2, wup_ref[:, cols])
        p0 = prev_s[0:1, cols]
        p1 = prev_s[1:2, cols]
        u1 = jnp.where(row == 0, p1, pltpu.roll(u, 1, axis=0))
        u2 = jnp.where(row == 0, p0, jnp.where(row == 1, p1, pltpu.roll(u, 2, axis=0)))
        prev_s[0:2, cols] = u[tm - 2:tm, :]
        return cb_ref[:, cols] + cw_ref[0:1, cols] * u2 + cw_ref[1:2, cols] * u1 + cw_ref[2:3, cols] * u

    acc = jnp.zeros((tm, D_MODEL), F32)
    for j in range(D_FF // FF_CHUNK):
        gate = conv(slice(j * FF_CHUNK, (j + 1) * FF_CHUNK))
        val = conv(slice(D_FF + j * FF_CHUNK, D_FF + (j + 1) * FF_CHUNK))
        act = (gate * jax.nn.sigmoid(gate) * val).astype(BF16)
        acc = acc + _dot(act, wdn_ref[j * FF_CHUNK:(j + 1) * FF_CHUNK, :])
    y_ref[...] = _rms(x1 + gtf_ref[0] * acc, gfin_ref[...])
    conv_ref[0] = prev_s[0:2, :]


def _ffn_prompt(x2d, o_r, o_f, mod3, conv0, fp, nb, t_len):
    tm = FFN_TILE
    tpb = t_len // tm
    rows = nb * t_len
    row_spec = lambda n: pl.BlockSpec((tm, n), lambda i: (i, 0))
    mod_spec = lambda j: pl.BlockSpec((1, 1, D_MODEL), lambda i: (i // tpb, 0, j))
    const = lambda shape: pl.BlockSpec(shape, lambda i: (0, 0))
    weight = lambda shape: pl.BlockSpec(shape, lambda i: (0, 0), pipeline_mode=pl.Buffered(1))
    conv_spec = pl.BlockSpec((1, CONV_W - 1, 2 * D_FF), lambda i: (i // tpb, 0, 0))
    return pl.pallas_call(
        functools.partial(_ffn_prompt_kernel, tpb), grid=(rows // tm,),
        in_specs=[row_spec(D_MODEL), row_spec(D_GROUP), row_spec(D_GROUP),
                  mod_spec(2), mod_spec(3), mod_spec(4), mod_spec(5),
                  weight((D_MODEL, D_MODEL)), const((1, D_MODEL)), weight((D_MODEL, 2 * D_FF)),
                  const((CONV_W, 2 * D_FF)), const((1, 2 * D_FF)), weight((D_FF, D_MODEL)),
                  const((1, D_MODEL)), conv_spec],
        out_specs=[row_spec(D_MODEL), conv_spec],
        out_shape=[jax.ShapeDtypeStruct((rows, D_MODEL), F32),
                   jax.ShapeDtypeStruct((nb, CONV_W - 1, 2 * D_FF), F32)],
        scratch_shapes=[pltpu.VMEM((8, 2 * D_FF), F32)],
        compiler_params=_params(dimension_semantics=("arbitrary",)), name="ffn_prompt")(
            x2d, o_r, o_f, mod3, mod3, mod3, mod3, fp["w_out"], fp["g_ffn"], fp["w_up"], fp["conv_w"],
            fp["conv_b"], fp["w_down"], fp["g_final"], conv0)


def _ffn_sample_kernel(x_ref, ort_ref, of_ref, gta_ref, shf_ref, scf_ref, gtf_ref, wout_ref, gffn_ref,
                       wup_ref, cw_ref, cb_ref, wdn_ref, gfin_ref, cp0_ref, cp1_ref, y_ref, u_ref):
    attn = (_dot(ort_ref[...].T.astype(BF16), wout_ref[0:D_GROUP, :])
            + _dot(of_ref[...].astype(BF16), wout_ref[D_GROUP:2 * D_GROUP, :]))
    x1 = x_ref[...] + gta_ref[...] * attn
    h2 = (_rms(x1, gffn_ref[...]) * (1 + scf_ref[...]) + shf_ref[...]).astype(BF16)

    def conv(cols):
        u = _dot(h2, wup_ref[:, cols])
        u_ref[:, cols] = u
        return (cb_ref[:, cols] + cw_ref[0:1, cols] * cp0_ref[:, cols] + cw_ref[1:2, cols] * cp1_ref[:, cols]
                + cw_ref[2:3, cols] * u)

    acc = jnp.zeros(x1.shape, F32)
    for j in range(D_FF // FF_CHUNK):
        gate = conv(slice(j * FF_CHUNK, (j + 1) * FF_CHUNK))
        val = conv(slice(D_FF + j * FF_CHUNK, D_FF + (j + 1) * FF_CHUNK))
        act = (gate * jax.nn.sigmoid(gate) * val).astype(BF16)
        acc = acc + _dot(act, wdn_ref[j * FF_CHUNK:(j + 1) * FF_CHUNK, :])
    y_ref[...] = _rms(x1 + gtf_ref[...] * acc, gfin_ref[...])


def _ffn_sample(x2d, o_r_t, o_f, mod, cp0, cp1, fp):
    m = x2d.shape[0]
    full = lambda n: pl.BlockSpec((m, n), lambda i: (0, 0))
    mod_spec = lambda j: pl.BlockSpec((m, D_MODEL), lambda i: (0, j))
    const = lambda shape: pl.BlockSpec(shape, lambda i: (0, 0))
    return pl.pallas_call(
        _ffn_sample_kernel, grid=(1,),
        in_specs=[full(D_MODEL), const((D_GROUP, m)), full(D_GROUP),
                  mod_spec(2), mod_spec(3), mod_spec(4), mod_spec(5),
                  const((D_MODEL, D_MODEL)), const((1, D_MODEL)), const((D_MODEL, 2 * D_FF)),
                  const((CONV_W, 2 * D_FF)), const((1, 2 * D_FF)), const((D_FF, D_MODEL)),
                  const((1, D_MODEL)), full(2 * D_FF), full(2 * D_FF)],
        out_specs=[full(D_MODEL), full(2 * D_FF)],
        out_shape=[jax.ShapeDtypeStruct((m, D_MODEL), F32), jax.ShapeDtypeStruct((m, 2 * D_FF), F32)],
        compiler_params=_params(), name="ffn_sample")(
            x2d, o_r_t, o_f, mod, mod, mod, mod, fp["w_out"], fp["g_ffn"], fp["w_up"], fp["conv_w"],
            fp["conv_b"], fp["w_down"], fp["g_final"], cp0, cp1)


def _rwkv_prep_kernel(zt_ref, sht_ref, mu_ref, w0_ref, wupt_ref, a0_ref, aupt_ref, gupt_ref, kk_ref, ka_ref,
                      r_ref, k_ref, v_ref, d_ref, kko_ref, a_ref, g_ref):
    z = zt_ref[...]
    zm = z + mu_ref[...] * (sht_ref[...] - z)
    k = zm[D_GROUP:2 * D_GROUP, :]
    lora = zm[LORA_OFF:ZR_PAD, :]
    w = w0_ref[...] + _dot(wupt_ref[...], jnp.tanh(lora).astype(BF16))
    a = jax.nn.sigmoid(a0_ref[...] + _dot(aupt_ref[...], lora.astype(BF16)))
    r_ref[...] = zm[0:D_GROUP, :]
    k_ref[...] = k * (1 + (a - 1) * ka_ref[...])
    v_ref[...] = zm[2 * D_GROUP:3 * D_GROUP, :]
    d_ref[...] = jnp.exp(-jnp.exp(-jax.nn.softplus(-w) - 0.5))
    kko_ref[...] = k * kk_ref[...]
    a_ref[...] = a
    g_ref[...] = _dot(gupt_ref[...], jax.nn.sigmoid(lora).astype(BF16))


def _rwkv_prep(zrt, shift_t, rp):
    m = zrt.shape[1]
    return pl.pallas_call(
        _rwkv_prep_kernel,
        out_shape=[jax.ShapeDtypeStruct((D_GROUP, m), F32)] * 7,
        compiler_params=_params(), name="rwkv_prep")(
            zrt, shift_t, rp["mu_c"], rp["w0_c"], rp["wup_t"], rp["a0_c"], rp["aup_t"], rp["gup_t"],
            rp["k_k_c"], rp["k_a_c"])


def _rwkv_step_kernel(r_ref, k_ref, v_ref, d_ref, kk_ref, a_ref, g_ref, rk_ref, lnw_ref, lnb_ref, s_ref,
                      o_ref, so_ref, y_s):
    r, k, d, kk, a = (ref[0] for ref in (r_ref, k_ref, d_ref, kk_ref, a_ref))
    nrm = jnp.sqrt(jnp.sum(kk * kk, axis=0, keepdims=True))
    kkn = kk / jnp.maximum(nrm, 1e-12)
    bb = kkn * a

    def body(i, carry):
        s = s_ref[0, i]
        sa = -jnp.sum(s * kkn, axis=0, keepdims=True)
        s = s * d + sa * bb + v_ref[0, pl.ds(i, 1), :] * k
        so_ref[0, i] = s
        y_s[pl.ds(i, 1), :] = jnp.sum(s * r, axis=0, keepdims=True)
        return carry

    lax.fori_loop(0, HEAD_DIM, body, 0)
    y = y_s[...]
    mean = jnp.mean(y, axis=0, keepdims=True)
    var = jnp.mean(jnp.square(y - mean), axis=0, keepdims=True)
    yn = (y - mean) * lax.rsqrt(var + GN_EPS)
    bonus = jnp.sum(r * k * rk_ref[0], axis=0, keepdims=True) * v_ref[0]
    o_ref[0] = (yn * lnw_ref[0] + lnb_ref[0] + bonus) * g_ref[0]


def _rwkv_step(vecs, rp, state_t):
    m = state_t.shape[-1]
    vec_spec = pl.BlockSpec((1, HEAD_DIM, m), lambda h: (h, 0, 0))
    par_spec = pl.BlockSpec((1, HEAD_DIM, 1), lambda h: (h, 0, 0))
    st_spec = pl.BlockSpec((1, HEAD_DIM, HEAD_DIM, m), lambda h: (h, 0, 0, 0))
    return pl.pallas_call(
        _rwkv_step_kernel, grid=(N_HEADS,),
        in_specs=[vec_spec] * 7 + [par_spec] * 3 + [st_spec],
        out_specs=[vec_spec, st_spec],
        out_shape=[jax.ShapeDtypeStruct((N_HEADS, HEAD_DIM, m), F32),
                   jax.ShapeDtypeStruct(state_t.shape, F32)],
        scratch_shapes=[pltpu.VMEM((HEAD_DIM, m), F32)],
        compiler_params=_params(dimension_semantics=("parallel",)), name="rwkv_step")(
            *vecs, rp["r_k_c"], rp["ln_w_c"], rp["ln_b_c"], state_t)


def _lane_prefix(x):
    lane = lax.broadcasted_iota(jnp.int32, x.shape, 1)
    s = 1
    while s < x.shape[1]:
        x = x + jnp.where(lane >= s, pltpu.roll(x, s, axis=1), 0.0)
        s *= 2
    return x


def _pattn_kernel(npg, pt_ref, q_ref, kn_ref, vn_ref, lfn_ref, *refs):
    pg = PAGES_PER_STEP
    kc_ref, vc_ref, lfc_ref, o_ref, kbuf, vbuf, lfbuf, sem, m_s, l_s, c_s, qc_s, acc_s = refs
    g = pl.program_id(1)
    n_seq = pl.num_programs(0)
    step = pl.program_id(0) * npg + g
    last = n_seq * npg - 1
    slot = step % PAGE_SLOTS
    ahead = PAGE_SLOTS - 1

    def page_copies(st, sl, fetch):
        seq, grp = st // npg, st % npg
        out = []
        for p in range(pg):
            page = pt_ref[grp * pg + p, seq] if fetch else 0
            out.append(pltpu.make_async_copy(kc_ref.at[page], kbuf.at[sl, p], sem.at[sl, 0]))
            out.append(pltpu.make_async_copy(vc_ref.at[page], vbuf.at[sl, p], sem.at[sl, 1]))
            out.append(pltpu.make_async_copy(lfc_ref.at[page], lfbuf.at[sl, p], sem.at[sl, 2]))
        return out

    @pl.when(step == 0)
    def _():
        for t in range(ahead):
            for cp in page_copies(jnp.minimum(t, last), t, True):
                cp.start()

    for cp in page_copies(step, slot, False):
        cp.wait()
    for cp in page_copies(jnp.minimum(step + ahead, last), (step + ahead) % PAGE_SLOTS, True):
        cp.start()

    n = HEAD_DIM
    eye = lax.broadcasted_iota(jnp.int32, (n, n), 0) == lax.broadcasted_iota(jnp.int32, (n, n), 1)

    @pl.when(g == 0)
    def _():
        m_s[...] = jnp.full_like(m_s, -jnp.inf)
        l_s[...] = jnp.zeros_like(l_s)
        c_s[...] = jnp.zeros_like(c_s)
        acc_s[...] = jnp.zeros_like(acc_s)
        q = q_ref[0].astype(F32)
        for h in range(N_HEADS):
            col = jnp.sum(jnp.where(eye, q[h:h + 1, :], 0.0), axis=-1, keepdims=True)
            qc_s[h] = jnp.broadcast_to(col, (n, PAGE))

    within = _lane_prefix(lfbuf[slot].reshape(pg * N_HEADS, PAGE))
    run = c_s[:, 0:1]
    scores = []
    for p in range(pg):
        w_p = within[p * N_HEADS:(p + 1) * N_HEADS, :]
        rows = [jnp.sum(qc_s[h] * kbuf[slot, p, h], axis=0, keepdims=True) for h in range(N_HEADS)]
        scores.append(jnp.concatenate(rows, axis=0) - (run + w_p))
        run = run + w_p[:, PAGE - 1:PAGE]
    c_s[...] = jnp.broadcast_to(run, c_s.shape)
    m_old = m_s[:, 0:1]
    m_new = m_old
    for s in scores:
        m_new = jnp.maximum(m_new, jnp.max(s, axis=-1, keepdims=True))
    alpha = jnp.exp(m_old - m_new)
    probs = [jnp.exp(s - m_new) for s in scores]
    l = alpha * l_s[:, 0:1]
    for pr in probs:
        l = l + jnp.sum(pr, axis=-1, keepdims=True)
    m_s[...] = jnp.broadcast_to(m_new, m_s.shape)
    l_s[...] = jnp.broadcast_to(l, l_s.shape)
    for h in range(N_HEADS):
        a = acc_s[h] * alpha[h:h + 1, :]
        for p in range(pg):
            a = a + probs[p][h:h + 1, :] * vbuf[slot, p, h]
        acc_s[h] = a

    @pl.when(step == last)
    def _():
        for t in range(1, PAGE_SLOTS):
            for cp in page_copies(step, (step + t) % PAGE_SLOTS, False):
                cp.wait()

    @pl.when(g == npg - 1)
    def _():
        rows = []
        for h in range(N_HEADS):
            col = jnp.sum(acc_s[h], axis=-1, keepdims=True)
            rows.append(jnp.sum(jnp.where(eye, col, 0.0), axis=0, keepdims=True))
        o_past = jnp.concatenate(rows, axis=0)
        r8 = lax.broadcasted_iota(jnp.int32, (N_HEADS, N_HEADS), 0)
        c8 = lax.broadcasted_iota(jnp.int32, (N_HEADS, N_HEADS), 1)
        lfn_col = jnp.sum(jnp.where(r8 == c8, lfn_ref[0], 0.0), axis=-1, keepdims=True)
        s_new = jnp.sum(q_ref[0].astype(F32) * kn_ref[0], axis=-1, keepdims=True) - (run + lfn_col)
        m_fin = jnp.maximum(m_new, s_new)
        beta = jnp.exp(m_new - m_fin)
        p_new = jnp.exp(s_new - m_fin)
        o_ref[0] = (o_past * beta + p_new * vn_ref[0]) / (l * beta + p_new)


def _attn_sample(qb3, kn3, vn3, lfn3, kc, vc, lfc, pt_t):
    n_pages, nb = pt_t.shape
    pg = PAGES_PER_STEP
    npg = n_pages // pg
    vec_spec = pl.BlockSpec((1, N_HEADS, HEAD_DIM), lambda b, g, pt: (b, 0, 0))
    hbm = pl.BlockSpec(memory_space=pl.ANY)
    grid_spec = pltpu.PrefetchScalarGridSpec(
        num_scalar_prefetch=1, grid=(nb, npg),
        in_specs=[vec_spec, vec_spec, vec_spec, pl.BlockSpec((1, 1, N_HEADS), lambda b, g, pt: (b, 0, 0)),
                  hbm, hbm, hbm],
        out_specs=vec_spec,
        scratch_shapes=[pltpu.VMEM((PAGE_SLOTS, pg, N_HEADS, HEAD_DIM, PAGE), F32),
                        pltpu.VMEM((PAGE_SLOTS, pg, N_HEADS, HEAD_DIM, PAGE), F32),
                        pltpu.VMEM((PAGE_SLOTS, pg, N_HEADS, PAGE), F32),
                        pltpu.SemaphoreType.DMA((PAGE_SLOTS, 3)),
                        pltpu.VMEM((N_HEADS, LANES), F32), pltpu.VMEM((N_HEADS, LANES), F32),
                        pltpu.VMEM((N_HEADS, LANES), F32),
                        pltpu.VMEM((N_HEADS, HEAD_DIM, PAGE), F32), pltpu.VMEM((N_HEADS, HEAD_DIM, PAGE), F32)])
    return pl.pallas_call(
        functools.partial(_pattn_kernel, npg), grid_spec=grid_spec,
        out_shape=jax.ShapeDtypeStruct((nb, N_HEADS, HEAD_DIM), F32),
        compiler_params=_params(dimension_semantics=("arbitrary", "arbitrary")), name="fox_sample")(
            pt_t, qb3, kn3, vn3, lfn3, kc, vc, lfc)


def _pad_cols(a, n):
    return jnp.pad(a, ((0, 0), (0, n - a.shape[1])))


def kernel(x_prompt, x_sample, c_prompt, c_sample, cache_k, cache_v, cache_logf, page_table, state_wkv,
           state_shift, state_ffn_conv, w_ada, b_ada, g_attn_norm, w_in, b_forget, rwkv_mu, rwkv_w0,
           rwkv_w_up, rwkv_a0, rwkv_a_up, rwkv_g_up, rwkv_k_k, rwkv_k_a, rwkv_r_k, rwkv_ln_w, rwkv_ln_b,
           w_out, g_ffn_norm, w_ffn_up, ffn_conv_w, ffn_conv_b, w_ffn_down, g_final_norm):
    depth = w_in.shape[0]
    assert depth == 1, "one layer per call"
    nb, t_len, _ = x_prompt.shape
    db, dt, _ = x_sample.shape
    assert dt == 1, "the sample group decodes one token per sequence"
    assert t_len % ATTN_TILE == 0 and db % 8 == 0

    w_ada_bf = w_ada[0].astype(BF16)
    b_ada2 = b_ada[0][None, :]
    wi = w_in[0]
    o = D_RWKV_IN
    w_in_bf = jnp.concatenate(
        [_pad_cols(wi[:, :o], ZR_PAD), wi[:, o:o + 3 * D_GROUP], _pad_cols(wi[:, o + 3 * D_GROUP:], F_PAD)],
        axis=1).astype(BF16)
    bfp = _pad_cols(b_forget[0][None, :], F_PAD)
    g_attn = g_attn_norm[0][None, :]
    lora_rows = lambda w, off: jnp.pad(w, ((off, LORA_W - off - w.shape[0]), (0, 0))).astype(BF16)
    row = lambda a: a[None, :]
    col = lambda a: a[:, None]
    head_col = lambda a: a.reshape(N_HEADS, HEAD_DIM, 1)
    rp = dict(mu=_pad_cols(row(rwkv_mu[0]), ZR_PAD), w0=row(rwkv_w0[0]), a0=row(rwkv_a0[0]),
              wup=lora_rows(rwkv_w_up[0], 0), aup=lora_rows(rwkv_a_up[0], W_LORA),
              gup=lora_rows(rwkv_g_up[0], W_LORA + A_LORA),
              k_k=row(rwkv_k_k[0]), k_a=row(rwkv_k_a[0]), ln_w=row(rwkv_ln_w[0]), ln_b=row(rwkv_ln_b[0]),
              r_k3=rwkv_r_k[0][:, None, :])
    rp.update(mu_c=rp["mu"].T, w0_c=col(rwkv_w0[0]), a0_c=col(rwkv_a0[0]), k_k_c=col(rwkv_k_k[0]),
              k_a_c=col(rwkv_k_a[0]), wup_t=rp["wup"].T, aup_t=rp["aup"].T, gup_t=rp["gup"].T,
              r_k_c=head_col(rwkv_r_k[0]), ln_w_c=head_col(rwkv_ln_w[0]), ln_b_c=head_col(rwkv_ln_b[0]))
    fp = dict(w_out=w_out[0].astype(BF16), g_ffn=row(g_ffn_norm[0]), w_up=w_ffn_up[0].astype(BF16),
              conv_w=ffn_conv_w[0], conv_b=row(ffn_conv_b[0]), w_down=w_ffn_down[0].astype(BF16),
              g_final=row(g_final_norm))

    rows = nb * t_len
    xp = x_prompt.reshape(rows, D_MODEL)
    mod_p = _mod(jnp.pad(c_prompt, ((0, 8 - nb), (0, 0))), w_ada_bf, b_ada2).reshape(8, 1, 6 * D_MODEL)
    zr, qb, kb, vb, kt_p, vt_p, lft = _inproj_prompt(xp, mod_p, nb, t_len, g_attn, w_in_bf, bfp)
    cum = _cumsum(lft)
    o_r, st = _rwkv_prompt(zr, jnp.zeros((nb, 1, ZR_PAD), F32), nb, t_len, rp)
    o_f = _attn_prompt(qb, kb, vb, cum, nb, t_len)
    y_p, conv_p = _ffn_prompt(xp, o_r, o_f, mod_p, jnp.zeros((nb, CONV_W - 1, 2 * D_FF), F32), fp, nb, t_len)

    heads_t = lambda a: jnp.transpose(a.reshape(nb, N_HEADS, HEAD_DIM, t_len), (0, 3, 1, 2))[None]
    y_prompt = y_p.reshape(nb, t_len, D_MODEL)
    k_prompt = heads_t(kt_p)
    v_prompt = heads_t(vt_p)
    logf_prompt = jnp.transpose(lft, (0, 2, 1))[None]
    wkv_prompt = jnp.swapaxes(st, -1, -2).reshape(1, nb, N_HEADS, HEAD_DIM, HEAD_DIM)
    shift_prompt = zr.reshape(nb, t_len, ZR_PAD)[:, -1, :D_RWKV_IN][None]
    conv_prompt = conv_p[None]

    xs = x_sample.reshape(db, D_MODEL)
    mod_s = _mod(c_sample, w_ada_bf, b_ada2)
    zrt_s, qb_s, k_s, v_s, kt_s, vt_s, lft_s = _inproj_sample(xs, mod_s, g_attn, w_in_bf, bfp)
    shift_t = jnp.pad(state_shift[0].T, ((0, ZR_PAD - D_RWKV_IN), (0, 0)))
    vecs = _rwkv_prep(zrt_s, shift_t, rp)
    hd3 = lambda a: a.reshape(N_HEADS, HEAD_DIM, db)
    o_r_t, wkv_t = _rwkv_step([hd3(a) for a in vecs], rp, jnp.transpose(state_wkv[0], (1, 2, 3, 0)))
    kc = jnp.transpose(cache_k[0], (0, 2, 3, 1))
    vc = jnp.transpose(cache_v[0], (0, 2, 3, 1))
    lfc = jnp.transpose(cache_logf[0], (0, 2, 1))
    to3 = lambda a: a.reshape(db, N_HEADS, HEAD_DIM)
    o_f_s = _attn_sample(to3(qb_s), to3(k_s), to3(v_s), lft_s.T.reshape(db, 1, N_HEADS), kc, vc, lfc,
                         page_table.T)
    cp = state_ffn_conv[0]
    y_s, u_s = _ffn_sample(xs, o_r_t.reshape(D_GROUP, db), o_f_s.reshape(db, D_GROUP), mod_s,
                           cp[:, 0, :], cp[:, 1, :], fp)

    heads_s = lambda a: jnp.transpose(hd3(a), (2, 0, 1)).reshape(1, db, 1, N_HEADS, HEAD_DIM)
    y_sample = y_s.reshape(db, 1, D_MODEL)
    k_sample = heads_s(kt_s)
    v_sample = heads_s(vt_s)
    logf_sample = lft_s.T.reshape(1, db, 1, N_HEADS)
    wkv_sample = jnp.transpose(wkv_t, (3, 0, 1, 2))[None]
    shift_sample = zrt_s[:D_RWKV_IN].T[None]
    conv_sample = jnp.stack([cp[:, 1, :], u_s], axis=1)[None]
    return (y_prompt, y_sample, k_prompt, v_prompt, logf_prompt, wkv_prompt, shift_prompt, conv_prompt,
            k_sample, v_sample, logf_sample, wkv_sample, shift_sample, conv_sample)
```

```python
import functools

import jax
import jax.numpy as jnp
from jax import lax
from jax.experimental import pallas as pl
from jax.experimental.pallas import tpu as pltpu

F32 = jnp.float32
BF16 = jnp.bfloat16

D_MODEL = 1024
HEAD_DIM = 64
N_HEADS = 8
D_GROUP = N_HEADS * HEAD_DIM
W_LORA, A_LORA, G_LORA = 32, 32, 96
D_RWKV_IN = 3 * D_GROUP + W_LORA + A_LORA + G_LORA
ZR_PAD = 1792
LORA_OFF = 3 * D_GROUP
LORA_W = ZR_PAD - LORA_OFF
F_PAD = 128
W_IN_PAD = ZR_PAD + 3 * D_GROUP + F_PAD
D_FF = 2816
CONV_W = 3
PAGE = 128
RMS_EPS = 1e-6
GN_EPS = 64e-5
ATTN_SCALE = HEAD_DIM ** -0.5
LANES = 128
VMEM_LIMIT = 56 * 1024 * 1024

ROW_TILE = 256
RWKV_TILE = 256
CHUNK = 64
ATTN_TILE = 512
ATTN_KEYS = 1024
FF_CHUNK = D_FF
PAGES_PER_STEP = 16
PAGE_SLOTS = 3
FFN_TILE = 512


def _params(**kw):
    return pltpu.CompilerParams(vmem_limit_bytes=VMEM_LIMIT, **kw)


def _dot(a, b):
    return jnp.dot(a, b, preferred_element_type=F32)


def _dot_nt(a, b):
    return lax.dot_general(a, b, (((1,), (1,)), ((), ())), preferred_element_type=F32)


def _dot_tn(a, b):
    return lax.dot_general(a, b, (((0,), (0,)), ((), ())), preferred_element_type=F32)


def _bdot(a, b):
    return lax.dot_general(a, b, (((2,), (1,)), ((0,), (0,))), preferred_element_type=F32)


def _bdot_nt(a, b):
    return lax.dot_general(a, b, (((2,), (2,)), ((0,), (0,))), preferred_element_type=F32)


def _bdot_tn(a, b):
    return lax.dot_general(a, b, (((1,), (1,)), ((0,), (0,))), preferred_element_type=F32)


def _split3(x):
    hi = x.astype(BF16)
    r1 = x - hi.astype(F32)
    mid = r1.astype(BF16)
    lo = (r1 - mid.astype(F32)).astype(BF16)
    return hi, mid, lo


def _rms(x, g):
    return x * lax.rsqrt(jnp.mean(x * x, axis=-1, keepdims=True) + RMS_EPS) * g


def _mod_kernel(c_ref, w_ref, b_ref, o_ref):
    c = c_ref[...]
    s = (c * jax.nn.sigmoid(c)).astype(BF16)
    o_ref[...] = _dot(s, w_ref[...]) + b_ref[...]


def _mod(c, w_bf, b):
    m, n, tn = c.shape[0], w_bf.shape[1], 1536
    return pl.pallas_call(
        _mod_kernel, grid=(n // tn,),
        in_specs=[pl.BlockSpec((m, D_MODEL), lambda j: (0, 0)),
                  pl.BlockSpec((D_MODEL, tn), lambda j: (0, j)),
                  pl.BlockSpec((1, tn), lambda j: (0, j))],
        out_specs=pl.BlockSpec((m, tn), lambda j: (0, j)),
        out_shape=jax.ShapeDtypeStruct((m, n), F32),
        compiler_params=_params(), name="adaln_mod")(c, w_bf, b)


def _project(per_row, x_ref, sh_ref, sc_ref, g_ref, w_ref, bf_ref):
    sh = sh_ref[...] if per_row else sh_ref[0]
    sc = sc_ref[...] if per_row else sc_ref[0]
    h = (_rms(x_ref[...], g_ref[...]) * (1 + sc) + sh).astype(BF16)
    o = ZR_PAD
    zr = _dot(h, w_ref[:, 0:o])
    q = _dot(h, w_ref[:, o:o + D_GROUP]) * ATTN_SCALE
    k = _dot(h, w_ref[:, o + D_GROUP:o + 2 * D_GROUP])
    v = _dot(h, w_ref[:, o + 2 * D_GROUP:o + 3 * D_GROUP])
    f = _dot(h, w_ref[:, o + 3 * D_GROUP:o + 3 * D_GROUP + F_PAD])
    return zr, q, k, v, jax.nn.log_sigmoid(f + bf_ref[...])


def _inproj_prompt_kernel(x_ref, sh_ref, sc_ref, g_ref, w_ref, bf_ref,
                          zr_ref, qb_ref, kb_ref, vb_ref, kt_ref, vt_ref, lft_ref, k_s, v_s):
    zr, q, k, v, lf = _project(False, x_ref, sh_ref, sc_ref, g_ref, w_ref, bf_ref)
    zr_ref[...] = zr
    qb_ref[...] = q.astype(BF16)
    kb_ref[...] = k.astype(BF16)
    vb_ref[...] = v.astype(BF16)
    k_s[...] = k
    v_s[...] = v
    kt_ref[0] = k_s[...].T
    vt_ref[0] = v_s[...].T
    lft_ref[0] = lf.T[0:N_HEADS, :]


def _inproj_sample_kernel(x_ref, sh_ref, sc_ref, g_ref, w_ref, bf_ref,
                          zrt_ref, qb_ref, k_ref, v_ref, kt_ref, vt_ref, lft_ref, zr_s):
    zr, q, k, v, lf = _project(True, x_ref, sh_ref, sc_ref, g_ref, w_ref, bf_ref)
    qb_ref[...] = q.astype(BF16)
    k_ref[...] = k
    v_ref[...] = v
    zr_s[...] = zr
    zrt_ref[...] = zr_s[...].T
    kt_ref[...] = k_ref[...].T
    vt_ref[...] = v_ref[...].T
    lft_ref[...] = lf.T[0:N_HEADS, :]


def _inproj_prompt(x2d, mod3, nb, t_len, g, w_bf, bfp):
    tm = ROW_TILE
    tpb = t_len // tm
    rows = nb * t_len
    mod_spec = lambda j: pl.BlockSpec((1, 1, D_MODEL), lambda i: (i // tpb, 0, j))
    row_spec = lambda n: pl.BlockSpec((tm, n), lambda i: (i, 0))
    col_spec = lambda n: pl.BlockSpec((1, n, tm), lambda i: (i // tpb, 0, i % tpb))
    const = lambda shape: pl.BlockSpec(shape, lambda i: (0, 0))
    return pl.pallas_call(
        _inproj_prompt_kernel, grid=(rows // tm,),
        in_specs=[row_spec(D_MODEL), mod_spec(0), mod_spec(1), const((1, D_MODEL)),
                  const((D_MODEL, W_IN_PAD)), const((1, F_PAD))],
        out_specs=[row_spec(ZR_PAD), row_spec(D_GROUP), row_spec(D_GROUP), row_spec(D_GROUP),
                   col_spec(D_GROUP), col_spec(D_GROUP), col_spec(N_HEADS)],
        out_shape=[jax.ShapeDtypeStruct((rows, ZR_PAD), F32),
                   jax.ShapeDtypeStruct((rows, D_GROUP), BF16),
                   jax.ShapeDtypeStruct((rows, D_GROUP), BF16),
                   jax.ShapeDtypeStruct((rows, D_GROUP), BF16),
                   jax.ShapeDtypeStruct((nb, D_GROUP, t_len), F32),
                   jax.ShapeDtypeStruct((nb, D_GROUP, t_len), F32),
                   jax.ShapeDtypeStruct((nb, N_HEADS, t_len), F32)],
        scratch_shapes=[pltpu.VMEM((tm, D_GROUP), F32), pltpu.VMEM((tm, D_GROUP), F32)],
        compiler_params=_params(dimension_semantics=("parallel",)), name="in_proj")(
            x2d, mod3, mod3, g, w_bf, bfp)


def _inproj_sample(x2d, mod, g, w_bf, bfp):
    m = x2d.shape[0]
    full = lambda r, c: pl.BlockSpec((r, c), lambda i: (0, 0))
    mod_spec = lambda j: pl.BlockSpec((m, D_MODEL), lambda i: (0, j))
    return pl.pallas_call(
        _inproj_sample_kernel, grid=(1,),
        in_specs=[full(m, D_MODEL), mod_spec(0), mod_spec(1), full(1, D_MODEL), full(D_MODEL, W_IN_PAD),
                  full(1, F_PAD)],
        out_specs=[full(ZR_PAD, m), full(m, D_GROUP), full(m, D_GROUP), full(m, D_GROUP),
                   full(D_GROUP, m), full(D_GROUP, m), full(N_HEADS, m)],
        out_shape=[jax.ShapeDtypeStruct((ZR_PAD, m), F32),
                   jax.ShapeDtypeStruct((m, D_GROUP), BF16),
                   jax.ShapeDtypeStruct((m, D_GROUP), F32),
                   jax.ShapeDtypeStruct((m, D_GROUP), F32),
                   jax.ShapeDtypeStruct((D_GROUP, m), F32),
                   jax.ShapeDtypeStruct((D_GROUP, m), F32),
                   jax.ShapeDtypeStruct((N_HEADS, m), F32)],
        scratch_shapes=[pltpu.VMEM((m, ZR_PAD), F32)],
        compiler_params=_params(), name="in_proj_sample")(x2d, mod, mod, g, w_bf, bfp)


def _cumsum_kernel(x_ref, o_ref):
    nb, _, t_len = x_ref.shape
    blk = 256
    r = lax.broadcasted_iota(jnp.int32, (blk, blk), 0)
    c = lax.broadcasted_iota(jnp.int32, (blk, blk), 1)
    upper = (r <= c).astype(BF16)
    for b in range(nb):
        carry = jnp.zeros((N_HEADS, 1), F32)
        for j in range(t_len // blk):
            sl = slice(j * blk, (j + 1) * blk)
            hi, mid, lo = _split3(x_ref[b, :, sl])
            cs = (_dot(lo, upper) + _dot(mid, upper)) + _dot(hi, upper) + carry
            o_ref[b, :, sl] = cs
            carry = cs[:, blk - 1:blk]


def _cumsum(lft):
    return pl.pallas_call(
        _cumsum_kernel, out_shape=jax.ShapeDtypeStruct(lft.shape, F32),
        compiler_params=_params(), name="logf_cumsum")(lft)


def _rwkv_mix(zm, w0, wup, a0, aup, gup, k_k, k_a):
    r = zm[:, 0:D_GROUP]
    k = zm[:, D_GROUP:2 * D_GROUP]
    v = zm[:, 2 * D_GROUP:3 * D_GROUP]
    lora = zm[:, LORA_OFF:ZR_PAD]
    w = w0 + _dot(jnp.tanh(lora).astype(BF16), wup)
    logd = -jnp.exp(-jax.nn.softplus(-w) - 0.5)
    a = jax.nn.sigmoid(a0 + _dot(lora.astype(BF16), aup))
    g = _dot(jax.nn.sigmoid(lora).astype(BF16), gup)
    kk = k * k_k
    k = k * (1 + (a - 1) * k_a)
    return r, k, v, logd, kk, a, g


def _group_norm(y):
    mean = jnp.mean(y, axis=-1, keepdims=True)
    var = jnp.mean(jnp.square(y - mean), axis=-1, keepdims=True)
    return (y - mean) * lax.rsqrt(var + GN_EPS)


def _rwkv_kernel(z_ref, sh0_ref, mu_ref, w0_ref, wup_ref, a0_ref, aup_ref, gup_ref, kk_ref, ka_ref,
                 rk_ref, lnw_ref, lnb_ref, o_ref, st_ref, prev_s, st_s, hm_s, yn_s, bon_s):
    tt, c = RWKV_TILE, CHUNK
    t = pl.program_id(1)

    @pl.when(t == 0)
    def _():
        prev_s[0:1, :] = sh0_ref[0]
        st_s[...] = jnp.zeros_like(st_s)

    z = z_ref[...]
    row = lax.broadcasted_iota(jnp.int32, (tt, 1), 0)
    zprev = jnp.where(row == 0, prev_s[0:1, :], pltpu.roll(z, 1, axis=0))
    prev_s[0:1, :] = z[tt - 1:tt, :]
    zm = z + mu_ref[...] * (zprev - z)
    r, k, v, logd, kk, a, g = _rwkv_mix(zm, w0_ref[...], wup_ref[...], a0_ref[...], aup_ref[...],
                                        gup_ref[...], kk_ref[...], ka_ref[...])
    ri = lax.broadcasted_iota(jnp.int32, (tt, tt), 0)
    ci = lax.broadcasted_iota(jnp.int32, (tt, tt), 1)
    tri = ((ri // c == ci // c) & (ci <= ri)).astype(BF16)
    hi, mid, lo = _split3(logd)
    cl = (_dot(tri, lo) + _dot(tri, mid)) + _dot(tri, hi)
    for h in range(N_HEADS):
        sl = slice(h * HEAD_DIM, (h + 1) * HEAD_DIM)
        for n, val in enumerate((r, k, v, kk, a, logd, cl)):
            hm_s[n, h] = val[:, sl]

    rr = lax.broadcasted_iota(jnp.int32, (c, c), 0)
    cc = lax.broadcasted_iota(jnp.int32, (c, c), 1)
    strict = (cc < rr).astype(F32)
    eye = rr == cc
    r2 = lax.broadcasted_iota(jnp.int32, (c, 2 * c), 0)
    c2 = lax.broadcasted_iota(jnp.int32, (c, 2 * c), 1)
    sgn = jnp.where(c2 < c, jnp.where(c2 <= r2, 1.0, 0.0), jnp.where(c2 - c <= r2, -1.0, 0.0)).astype(F32)
    nch = tt // c
    grp = N_HEADS * nch
    r_, k_, v_, kk_, a_, ld_, cl_ = (hm_s[n].reshape(grp, c, HEAD_DIM) for n in range(7))
    nrm = jnp.sqrt(jnp.sum(kk_ * kk_, axis=-1, keepdims=True))
    kkn = kk_ / jnp.maximum(nrm, 1e-12)
    b_ = kkn * a_
    cl_end = cl_[:, c - 1:c, :]
    e_end = jnp.exp(cl_end - cl_)
    g_inv = jnp.exp(-cl_)
    kkg = kkn * jnp.exp(cl_ - ld_)
    rg = r_ * jnp.exp(cl_)
    kd = (k_ * g_inv).astype(BF16)
    bd = (b_ * g_inv).astype(BF16)
    kkg_b = kkg.astype(BF16)
    a_k = _bdot_nt(kkg_b, kd) * strict
    a_b = _bdot_nt(kkg_b, bd) * strict
    ll = _bdot_nt(rg.astype(BF16), jnp.concatenate([kd, bd], axis=1)) * sgn
    akv = _bdot(a_k.astype(BF16), v_.astype(BF16))
    x = jnp.concatenate([akv, kkg], axis=2)
    p = (-a_b).astype(BF16)
    x = x + _bdot(p, x.astype(BF16))
    n = 2
    while n < c:
        p = _bdot(p, p).astype(BF16)
        x = x + _bdot(p, x.astype(BF16))
        n *= 2
    rhs = jnp.concatenate([jnp.concatenate([v_, jnp.zeros_like(v_)], axis=2), x], axis=1).astype(BF16)
    yq = _bdot(ll.astype(BF16), rhs)
    kb = jnp.concatenate([k_ * e_end, -(b_ * e_end)], axis=1).astype(BF16)
    sw = _bdot_tn(kb, rhs)
    qc = (rg + yq[:, :, HEAD_DIM:]).astype(BF16)
    m = (jnp.where(eye, jnp.broadcast_to(jnp.exp(cl_end), (grp, c, c)), 0.0) + sw[:, :, HEAD_DIM:]).astype(BF16)
    by_head = lambda t: t.reshape(N_HEADS, nch, *t.shape[1:])
    qc, m, y0, w = by_head(qc), by_head(m), by_head(yq[:, :, 0:HEAD_DIM]), by_head(sw[:, :, 0:HEAD_DIM])
    st = st_s[...]
    ys = []
    for i in range(nch):
        st_b = st.astype(BF16)
        ys.append(_bdot(qc[:, i], st_b) + y0[:, i])
        st = _bdot(m[:, i], st_b) + w[:, i]
    st_s[...] = st
    yn_s[...] = _group_norm(jnp.concatenate(ys, axis=1))
    bon_s[...] = jnp.sum(hm_s[0] * hm_s[1] * rk_ref[...], axis=-1, keepdims=True) * hm_s[2]
    yn = jnp.concatenate([yn_s[h] for h in range(N_HEADS)], axis=1)
    bon = jnp.concatenate([bon_s[h] for h in range(N_HEADS)], axis=1)
    o_ref[...] = ((yn * lnw_ref[...] + lnb_ref[...] + bon) * g).astype(BF16)
    st_ref[...] = st_s[...]


def _rwkv_prompt(zr, shift0, nb, t_len, rp):
    tt = RWKV_TILE
    nt = t_len // tt
    const = lambda shape: pl.BlockSpec(shape, lambda b, t: (0,) * len(shape))
    return pl.pallas_call(
        _rwkv_kernel, grid=(nb, nt),
        in_specs=[pl.BlockSpec((tt, ZR_PAD), lambda b, t: (b * nt + t, 0)),
                  pl.BlockSpec((1, 1, ZR_PAD), lambda b, t: (b, 0, 0)),
                  const((1, ZR_PAD)), const((1, D_GROUP)), const((LORA_W, D_GROUP)),
                  const((1, D_GROUP)), const((LORA_W, D_GROUP)), const((LORA_W, D_GROUP)),
                  const((1, D_GROUP)), const((1, D_GROUP)), const((N_HEADS, 1, HEAD_DIM)),
                  const((1, D_GROUP)), const((1, D_GROUP))],
        out_specs=[pl.BlockSpec((tt, D_GROUP), lambda b, t: (b * nt + t, 0)),
                   pl.BlockSpec((N_HEADS, HEAD_DIM, HEAD_DIM), lambda b, t: (b, 0, 0))],
        out_shape=[jax.ShapeDtypeStruct((nb * t_len, D_GROUP), BF16),
                   jax.ShapeDtypeStruct((nb * N_HEADS, HEAD_DIM, HEAD_DIM), F32)],
        scratch_shapes=[pltpu.VMEM((8, ZR_PAD), F32),
                        pltpu.VMEM((N_HEADS, HEAD_DIM, HEAD_DIM), F32),
                        pltpu.VMEM((7, N_HEADS, tt, HEAD_DIM), F32),
                        pltpu.VMEM((N_HEADS, tt, HEAD_DIM), F32),
                        pltpu.VMEM((N_HEADS, tt, HEAD_DIM), F32)],
        compiler_params=_params(dimension_semantics=("arbitrary", "arbitrary")), name="rwkv_scan")(
            zr, shift0, rp["mu"], rp["w0"], rp["wup"], rp["a0"], rp["aup"], rp["gup"], rp["k_k"],
            rp["k_a"], rp["r_k3"], rp["ln_w"], rp["ln_b"])


def _attn_kernel(q_ref, k_ref, v_ref, c_ref, o_ref):
    tq, tk = ATTN_TILE, ATTN_KEYS
    qi = pl.program_id(2)
    q_start = qi * tq
    q2 = q_ref[...]
    lane = lax.broadcasted_iota(jnp.int32, (1, LANES), 1)
    lo = lane < HEAD_DIM
    zero = jnp.zeros_like(q2)
    q_h = (jnp.where(lo, q2, zero), jnp.where(lo, zero, q2))
    c_ref0 = c_ref[0, 0, :, pl.ds(pl.multiple_of(q_start, tq), LANES)][:, 0:1]

    def step(off, width, carry, masked):
        kb = k_ref[pl.ds(off, width), :]
        vb = v_ref[pl.ds(off, width), :]
        ck = c_ref[0, 0, :, pl.ds(off, width)]
        if masked:
            row = lax.broadcasted_iota(jnp.int32, (tq, width), 0)
            col = lax.broadcasted_iota(jnp.int32, (tq, width), 1)
            causal = col + off <= row + q_start
        out = []
        for hd in range(2):
            m, l, acc = carry[3 * hd:3 * hd + 3]
            s = _dot_nt(q_h[hd], kb) + (c_ref0[hd:hd + 1, :] - ck[hd:hd + 1, :])
            if masked:
                s = jnp.where(causal, s, -jnp.inf)
            m_new = jnp.maximum(m, jnp.max(s, axis=-1, keepdims=True))
            alpha = jnp.exp(m - m_new)
            p = jnp.exp(s - m_new)
            l = alpha * l + jnp.sum(p, axis=-1, keepdims=True)
            acc = alpha * acc + _dot(p.astype(BF16), vb)
            out += [m_new, l, acc]
        return tuple(out)

    init = (jnp.full((tq, 1), -jnp.inf, F32), jnp.zeros((tq, 1), F32), jnp.zeros((tq, LANES), F32)) * 2
    n_full = q_start // tk
    carry = lax.fori_loop(0, n_full, lambda j, cr: step(pl.multiple_of(j * tk, tk), tk, cr, False), init)
    tail = pl.multiple_of(n_full * tk, tk)
    pieces = (q_start + tq - tail) // tq
    carry = lax.switch(pieces - 1, [functools.partial(step, tail, (n + 1) * tq, masked=True)
                                    for n in range(tk // tq)], carry)
    _, l0, acc0, _, l1, acc1 = carry
    o_ref[...] = jnp.where(lo, acc0 / l0, acc1 / l1).astype(BF16)


def _attn_prompt(qb, kb, vb, cum, nb, t_len):
    tq = ATTN_TILE
    nq = t_len // tq
    n_pairs = D_GROUP // LANES
    cum4 = cum.reshape(nb, n_pairs, 2, t_len)
    return pl.pallas_call(
        _attn_kernel, grid=(nb, n_pairs, nq),
        in_specs=[pl.BlockSpec((tq, LANES), lambda b, hp, qi: (b * nq + qi, hp)),
                  pl.BlockSpec((t_len, LANES), lambda b, hp, qi: (b, hp)),
                  pl.BlockSpec((t_len, LANES), lambda b, hp, qi: (b, hp)),
                  pl.BlockSpec((1, 1, 2, t_len), lambda b, hp, qi: (b, hp, 0, 0))],
        out_specs=pl.BlockSpec((tq, LANES), lambda b, hp, qi: (b * nq + qi, hp)),
        out_shape=jax.ShapeDtypeStruct((nb * t_len, D_GROUP), BF16),
        compiler_params=_params(dimension_semantics=("parallel", "parallel", "arbitrary")),
        name="fox_prompt")(qb, kb, vb, cum4)


def _ffn_front(x, o_r, o_f, gta, shf, scf, wout_ref, gffn):
    attn = _dot(o_r, wout_ref[0:D_GROUP, :]) + _dot(o_f, wout_ref[D_GROUP:2 * D_GROUP, :])
    x1 = x + gta * attn
    return x1, (_rms(x1, gffn) * (1 + scf) + shf).astype(BF16)


def _ffn_prompt_kernel(tpb, x_ref, or_ref, of_ref, gta_ref, shf_ref, scf_ref, gtf_ref, wout_ref, gffn_ref,
                       wup_ref, cw_ref, cb_ref, wdn_ref, gfin_ref, cp_ref, y_ref, conv_ref, prev_s):
    i = pl.program_id(0)

    @pl.when(i % tpb == 0)
    def _():
        prev_s[0:2, :] = cp_ref[0]

    x1, h2 = _ffn_front(x_ref[...], or_ref[...], of_ref[...], gta_ref[0], shf_ref[0], scf_ref[0],
                        wout_ref, gffn_ref[...])
    tm = x1.shape[0]
    row = lax.broadcasted_iota(jnp.int32, (tm, 1), 0)

    def conv(cols):
        u = _dot(h2, wup_ref[:, cols])
        p0 = prev_s[0:1, cols]
        p1 = prev_s[1:2, cols]
        u1 = jnp.where(row == 0, p1, pltpu.roll(u, 1, axis=0))
        u2 = jnp.where(row == 0, p0, jnp.where(row == 1, p1, pltpu.roll(u, 2, axis=0)))
        prev_s[0:2, cols] = u[tm - 2:tm, :]
        return cb_ref[:, cols] + cw_ref[0:1, cols] * u2 + cw_ref[1:2, cols] * u1 + cw_ref[2:3, cols] * u

    acc = jnp.zeros((tm, D_MODEL), F32)
    for j in range(D_FF // FF_CHUNK):
        gate = conv(slice(j * FF_CHUNK, (j + 1) * FF_CHUNK))
        val = conv(slice(D_FF + j * FF_CHUNK, D_FF + (j + 1) * FF_CHUNK))
        act = (gate * jax.nn.sigmoid(gate) * val).astype(BF16)
        acc = acc + _dot(act, wdn_ref[j * FF_CHUNK:(j + 1) * FF_CHUNK, :])
    y_ref[...] = _rms(x1 + gtf_ref[0] * acc, gfin_ref[...])
    conv_ref[0] = prev_s[0:2, :]


def _ffn_prompt(x2d, o_r, o_f, mod3, conv0, fp, nb, t_len):
    tm = FFN_TILE
    tpb = t_len // tm
    rows = nb * t_len
    row_spec = lambda n: pl.BlockSpec((tm, n), lambda i: (i, 0))
    mod_spec = lambda j: pl.BlockSpec((1, 1, D_MODEL), lambda i: (i // tpb, 0, j))
    const = lambda shape: pl.BlockSpec(shape, lambda i: (0, 0))
    weight = lambda shape: pl.BlockSpec(shape, lambda i: (0, 0), pipeline_mode=pl.Buffered(1))
    conv_spec = pl.BlockSpec((1, CONV_W - 1, 2 * D_FF), lambda i: (i // tpb, 0, 0))
    return pl.pallas_call(
        functools.partial(_ffn_prompt_kernel, tpb), grid=(rows // tm,),
        in_specs=[row_spec(D_MODEL), row_spec(D_GROUP), row_spec(D_GROUP),
                  mod_spec(2), mod_spec(3), mod_spec(4), mod_spec(5),
                  weight((D_MODEL, D_MODEL)), const((1, D_MODEL)), weight((D_MODEL, 2 * D_FF)),
                  const((CONV_W, 2 * D_FF)), const((1, 2 * D_FF)), weight((D_FF, D_MODEL)),
                  const((1, D_MODEL)), conv_spec],
        out_specs=[row_spec(D_MODEL), conv_spec],
        out_shape=[jax.ShapeDtypeStruct((rows, D_MODEL), F32),
                   jax.ShapeDtypeStruct((nb, CONV_W - 1, 2 * D_FF), F32)],
        scratch_shapes=[pltpu.VMEM((8, 2 * D_FF), F32)],
        compiler_params=_params(dimension_semantics=("arbitrary",)), name="ffn_prompt")(
            x2d, o_r, o_f, mod3, mod3, mod3, mod3, fp["w_out"], fp["g_ffn"], fp["w_up"], fp["conv_w"],
            fp["conv_b"], fp["w_down"], fp["g_final"], conv0)


def _ffn_sample_kernel(x_ref, ort_ref, of_ref, gta_ref, shf_ref, scf_ref, gtf_ref, wout_ref, gffn_ref,
                       wup_ref, cw_ref, cb_ref, wdn_ref, gfin_ref, cp0_ref, cp1_ref, y_ref, u_ref):
    attn = (_dot(ort_ref[...].T.astype(BF16), wout_ref[0:D_GROUP, :])
            + _dot(of_ref[...].astype(BF16), wout_ref[D_GROUP:2 * D_GROUP, :]))
    x1 = x_ref[...] + gta_ref[...] * attn
    h2 = (_rms(x1, gffn_ref[...]) * (1 + scf_ref[...]) + shf_ref[...]).astype(BF16)

    def conv(cols):
        u = _dot(h2, wup_ref[:, cols])
        u_ref[:, cols] = u
        return (cb_ref[:, cols] + cw_ref[0:1, cols] * cp0_ref[:, cols] + cw_ref[1:2, cols] * cp1_ref[:, cols]
                + cw_ref[2:3, cols] * u)

    acc = jnp.zeros(x1.shape, F32)
    for j in range(D_FF // FF_CHUNK):
        gate = conv(slice(j * FF_CHUNK, (j + 1) * FF_CHUNK))
        val = conv(slice(D_FF + j * FF_CHUNK, D_FF + (j + 1) * FF_CHUNK))
        act = (gate * jax.nn.sigmoid(gate) * val).astype(BF16)
        acc = acc + _dot(act, wdn_ref[j * FF_CHUNK:(j + 1) * FF_CHUNK, :])
    y_ref[...] = _rms(x1 + gtf_ref[...] * acc, gfin_ref[...])


def _ffn_sample(x2d, o_r_t, o_f, mod, cp0, cp1, fp):
    m = x2d.shape[0]
    full = lambda n: pl.BlockSpec((m, n), lambda i: (0, 0))
    mod_spec = lambda j: pl.BlockSpec((m, D_MODEL), lambda i: (0, j))
    const = lambda shape: pl.BlockSpec(shape, lambda i: (0, 0))
    return pl.pallas_call(
        _ffn_sample_kernel, grid=(1,),
        in_specs=[full(D_MODEL), const((D_GROUP, m)), full(D_GROUP),
                  mod_spec(2), mod_spec(3), mod_spec(4), mod_spec(5),
                  const((D_MODEL, D_MODEL)), const((1, D_MODEL)), const((D_MODEL, 2 * D_FF)),
                  const((CONV_W, 2 * D_FF)), const((1, 2 * D_FF)), const((D_FF, D_MODEL)),
                  const((1, D_MODEL)), full(2 * D_FF), full(2 * D_FF)],
        out_specs=[full(D_MODEL), full(2 * D_FF)],
        out_shape=[jax.ShapeDtypeStruct((m, D_MODEL), F32), jax.ShapeDtypeStruct((m, 2 * D_FF), F32)],
        compiler_params=_params(), name="ffn_sample")(
            x2d, o_r_t, o_f, mod, mod, mod, mod, fp["w_out"], fp["g_ffn"], fp["w_up"], fp["conv_w"],
            fp["conv_b"], fp["w_down"], fp["g_final"], cp0, cp1)


def _rwkv_prep_kernel(zt_ref, sht_ref, mu_ref, w0_ref, wupt_ref, a0_ref, aupt_ref, gupt_ref, kk_ref, ka_ref,
                      r_ref, k_ref, v_ref, d_ref, kko_ref, a_ref, g_ref):
    z = zt_ref[...]
    zm = z + mu_ref[...] * (sht_ref[...] - z)
    k = zm[D_GROUP:2 * D_GROUP, :]
    lora = zm[LORA_OFF:ZR_PAD, :]
    w = w0_ref[...] + _dot(wupt_ref[...], jnp.tanh(lora).astype(BF16))
    a = jax.nn.sigmoid(a0_ref[...] + _dot(aupt_ref[...], lora.astype(BF16)))
    r_ref[...] = zm[0:D_GROUP, :]
    k_ref[...] = k * (1 + (a - 1) * ka_ref[...])
    v_ref[...] = zm[2 * D_GROUP:3 * D_GROUP, :]
    d_ref[...] = jnp.exp(-jnp.exp(-jax.nn.softplus(-w) - 0.5))
    kko_ref[...] = k * kk_ref[...]
    a_ref[...] = a
    g_ref[...] = _dot(gupt_ref[...], jax.nn.sigmoid(lora).astype(BF16))


def _rwkv_prep(zrt, shift_t, rp):
    m = zrt.shape[1]
    return pl.pallas_call(
        _rwkv_prep_kernel,
        out_shape=[jax.ShapeDtypeStruct((D_GROUP, m), F32)] * 7,
        compiler_params=_params(), name="rwkv_prep")(
            zrt, shift_t, rp["mu_c"], rp["w0_c"], rp["wup_t"], rp["a0_c"], rp["aup_t"], rp["gup_t"],
            rp["k_k_c"], rp["k_a_c"])


def _rwkv_step_kernel(r_ref, k_ref, v_ref, d_ref, kk_ref, a_ref, g_ref, rk_ref, lnw_ref, lnb_ref, s_ref,
                      o_ref, so_ref, y_s):
    r, k, d, kk, a = (ref[0] for ref in (r_ref, k_ref, d_ref, kk_ref, a_ref))
    nrm = jnp.sqrt(jnp.sum(kk * kk, axis=0, keepdims=True))
    kkn = kk / jnp.maximum(nrm, 1e-12)
    bb = kkn * a

    def body(i, carry):
        s = s_ref[0, i]
        sa = -jnp.sum(s * kkn, axis=0, keepdims=True)
        s = s * d + sa * bb + v_ref[0, pl.ds(i, 1), :] * k
        so_ref[0, i] = s
        y_s[pl.ds(i, 1), :] = jnp.sum(s * r, axis=0, keepdims=True)
        return carry

    lax.fori_loop(0, HEAD_DIM, body, 0)
    y = y_s[...]
    mean = jnp.mean(y, axis=0, keepdims=True)
    var = jnp.mean(jnp.square(y - mean), axis=0, keepdims=True)
    yn = (y - mean) * lax.rsqrt(var + GN_EPS)
    bonus = jnp.sum(r * k * rk_ref[0], axis=0, keepdims=True) * v_ref[0]
    o_ref[0] = (yn * lnw_ref[0] + lnb_ref[0] + bonus) * g_ref[0]


def _rwkv_step(vecs, rp, state_t):
    m = state_t.shape[-1]
    vec_spec = pl.BlockSpec((1, HEAD_DIM, m), lambda h: (h, 0, 0))
    par_spec = pl.BlockSpec((1, HEAD_DIM, 1), lambda h: (h, 0, 0))
    st_spec = pl.BlockSpec((1, HEAD_DIM, HEAD_DIM, m), lambda h: (h, 0, 0, 0))
    return pl.pallas_call(
        _rwkv_step_kernel, grid=(N_HEADS,),
        in_specs=[vec_spec] * 7 + [par_spec] * 3 + [st_spec],
        out_specs=[vec_spec, st_spec],
        out_shape=[jax.ShapeDtypeStruct((N_HEADS, HEAD_DIM, m), F32),
                   jax.ShapeDtypeStruct(state_t.shape, F32)],
        scratch_shapes=[pltpu.VMEM((HEAD_DIM, m), F32)],
        compiler_params=_params(dimension_semantics=("parallel",)), name="rwkv_step")(
            *vecs, rp["r_k_c"], rp["ln_w_c"], rp["ln_b_c"], state_t)


def _lane_prefix(x):
    lane = lax.broadcasted_iota(jnp.int32, x.shape, 1)
    s = 1
    while s < x.shape[1]:
        x = x + jnp.where(lane >= s, pltpu.roll(x, s, axis=1), 0.0)
        s *= 2
    return x


def _pattn_kernel(npg, pt_ref, q_ref, kn_ref, vn_ref, lfn_ref, *refs):
    pg = PAGES_PER_STEP
    kc_ref, vc_ref, lfc_ref, o_ref, kbuf, vbuf, lfbuf, sem, m_s, l_s, c_s, qc_s, acc_s = refs
    g = pl.program_id(1)
    n_seq = pl.num_programs(0)
    step = pl.program_id(0) * npg + g
    last = n_seq * npg - 1
    slot = step % PAGE_SLOTS
    ahead = PAGE_SLOTS - 1

    def page_copies(st, sl, fetch):
        seq, grp = st // npg, st % npg
        out = []
        for p in range(pg):
            page = pt_ref[grp * pg + p, seq] if fetch else 0
            out.append(pltpu.make_async_copy(kc_ref.at[page], kbuf.at[sl, p], sem.at[sl, 0]))
            out.append(pltpu.make_async_copy(vc_ref.at[page], vbuf.at[sl, p], sem.at[sl, 1]))
            out.append(pltpu.make_async_copy(lfc_ref.at[page], lfbuf.at[sl, p], sem.at[sl, 2]))
        return out

    @pl.when(step == 0)
    def _():
        for t in range(ahead):
            for cp in page_copies(jnp.minimum(t, last), t, True):
                cp.start()

    for cp in page_copies(step, slot, False):
        cp.wait()
    for cp in page_copies(jnp.minimum(step + ahead, last), (step + ahead) % PAGE_SLOTS, True):
        cp.start()

    n = HEAD_DIM
    eye = lax.broadcasted_iota(jnp.int32, (n, n), 0) == lax.broadcasted_iota(jnp.int32, (n, n), 1)

    @pl.when(g == 0)
    def _():
        m_s[...] = jnp.full_like(m_s, -jnp.inf)
        l_s[...] = jnp.zeros_like(l_s)
        c_s[...] = jnp.zeros_like(c_s)
        acc_s[...] = jnp.zeros_like(acc_s)
        q = q_ref[0].astype(F32)
        for h in range(N_HEADS):
            col = jnp.sum(jnp.where(eye, q[h:h + 1, :], 0.0), axis=-1, keepdims=True)
            qc_s[h] = jnp.broadcast_to(col, (n, PAGE))

    within = _lane_prefix(lfbuf[slot].reshape(pg * N_HEADS, PAGE))
    run = c_s[:, 0:1]
    scores = []
    for p in range(pg):
        w_p = within[p * N_HEADS:(p + 1) * N_HEADS, :]
        rows = [jnp.sum(qc_s[h] * kbuf[slot, p, h], axis=0, keepdims=True) for h in range(N_HEADS)]
        scores.append(jnp.concatenate(rows, axis=0) - (run + w_p))
        run = run + w_p[:, PAGE - 1:PAGE]
    c_s[...] = jnp.broadcast_to(run, c_s.shape)
    m_old = m_s[:, 0:1]
    m_new = m_old
    for s in scores:
        m_new = jnp.maximum(m_new, jnp.max(s, axis=-1, keepdims=True))
    alpha = jnp.exp(m_old - m_new)
    probs = [jnp.exp(s - m_new) for s in scores]
    l = alpha * l_s[:, 0:1]
    for pr in probs:
        l = l + jnp.sum(pr, axis=-1, keepdims=True)
    m_s[...] = jnp.broadcast_to(m_new, m_s.shape)
    l_s[...] = jnp.broadcast_to(l, l_s.shape)
    for h in range(N_HEADS):
        a = acc_s[h] * alpha[h:h + 1, :]
        for p in range(pg):
            a = a + probs[p][h:h + 1, :] * vbuf[slot, p, h]
        acc_s[h] = a

    @pl.when(step == last)
    def _():
        for t in range(1, PAGE_SLOTS):
            for cp in page_copies(step, (step + t) % PAGE_SLOTS, False):
                cp.wait()

    @pl.when(g == npg - 1)
    def _():
        rows = []
        for h in range(N_HEADS):
            col = jnp.sum(acc_s[h], axis=-1, keepdims=True)
            rows.append(jnp.sum(jnp.where(eye, col, 0.0), axis=0, keepdims=True))
        o_past = jnp.concatenate(rows, axis=0)
        r8 = lax.broadcasted_iota(jnp.int32, (N_HEADS, N_HEADS), 0)
        c8 = lax.broadcasted_iota(jnp.int32, (N_HEADS, N_HEADS), 1)
        lfn_col = jnp.sum(jnp.where(r8 == c8, lfn_ref[0], 0.0), axis=-1, keepdims=True)
        s_new = jnp.sum(q_ref[0].astype(F32) * kn_ref[0], axis=-1, keepdims=True) - (run + lfn_col)
        m_fin = jnp.maximum(m_new, s_new)
        beta = jnp.exp(m_new - m_fin)
        p_new = jnp.exp(s_new - m_fin)
        o_ref[0] = (o_past * beta + p_new * vn_ref[0]) / (l * beta + p_new)


def _attn_sample(qb3, kn3, vn3, lfn3, kc, vc, lfc, pt_t):
    n_pages, nb = pt_t.shape
    pg = PAGES_PER_STEP
    npg = n_pages // pg
    vec_spec = pl.BlockSpec((1, N_HEADS, HEAD_DIM), lambda b, g, pt: (b, 0, 0))
    hbm = pl.BlockSpec(memory_space=pl.ANY)
    grid_spec = pltpu.PrefetchScalarGridSpec(
        num_scalar_prefetch=1, grid=(nb, npg),
        in_specs=[vec_spec, vec_spec, vec_spec, pl.BlockSpec((1, 1, N_HEADS), lambda b, g, pt: (b, 0, 0)),
                  hbm, hbm, hbm],
        out_specs=vec_spec,
        scratch_shapes=[pltpu.VMEM((PAGE_SLOTS, pg, N_HEADS, HEAD_DIM, PAGE), F32),
                        pltpu.VMEM((PAGE_SLOTS, pg, N_HEADS, HEAD_DIM, PAGE), F32),
                        pltpu.VMEM((PAGE_SLOTS, pg, N_HEADS, PAGE), F32),
                        pltpu.SemaphoreType.DMA((PAGE_SLOTS, 3)),
                        pltpu.VMEM((N_HEADS, LANES), F32), pltpu.VMEM((N_HEADS, LANES), F32),
                        pltpu.VMEM((N_HEADS, LANES), F32),
                        pltpu.VMEM((N_HEADS, HEAD_DIM, PAGE), F32), pltpu.VMEM((N_HEADS, HEAD_DIM, PAGE), F32)])
    return pl.pallas_call(
        functools.partial(_pattn_kernel, npg), grid_spec=grid_spec,
        out_shape=jax.ShapeDtypeStruct((nb, N_HEADS, HEAD_DIM), F32),
        compiler_params=_params(dimension_semantics=("arbitrary", "arbitrary")), name="fox_sample")(
            pt_t, qb3, kn3, vn3, lfn3, kc, vc, lfc)


def _pad_cols(a, n):
    return jnp.pad(a, ((0, 0), (0, n - a.shape[1])))


def kernel(x_prompt, x_sample, c_prompt, c_sample, cache_k, cache_v, cache_logf, page_table, state_wkv,
           state_shift, state_ffn_conv, w_ada, b_ada, g_attn_norm, w_in, b_forget, rwkv_mu, rwkv_w0,
           rwkv_w_up, rwkv_a0, rwkv_a_up, rwkv_g_up, rwkv_k_k, rwkv_k_a, rwkv_r_k, rwkv_ln_w, rwkv_ln_b,
           w_out, g_ffn_norm, w_ffn_up, ffn_conv_w, ffn_conv_b, w_ffn_down, g_final_norm):
    depth = w_in.shape[0]
    assert depth == 1, "one layer per call"
    nb, t_len, _ = x_prompt.shape
    db, dt, _ = x_sample.shape
    assert dt == 1, "the sample group decodes one token per sequence"
    assert t_len % FFN_TILE == 0 and t_len % ATTN_TILE == 0 and ATTN_KEYS % ATTN_TILE == 0 and db % 8 == 0

    w_ada_bf = w_ada[0].astype(BF16)
    b_ada2 = b_ada[0][None, :]
    wi = w_in[0]
    o = D_RWKV_IN
    w_in_bf = jnp.concatenate(
        [_pad_cols(wi[:, :o], ZR_PAD), wi[:, o:o + 3 * D_GROUP], _pad_cols(wi[:, o + 3 * D_GROUP:], F_PAD)],
        axis=1).astype(BF16)
    bfp = _pad_cols(b_forget[0][None, :], F_PAD)
    g_attn = g_attn_norm[0][None, :]
    lora_rows = lambda w, off: jnp.pad(w, ((off, LORA_W - off - w.shape[0]), (0, 0))).astype(BF16)
    row = lambda a: a[None, :]
    col = lambda a: a[:, None]
    head_col = lambda a: a.reshape(N_HEADS, HEAD_DIM, 1)
    rp = dict(mu=_pad_cols(row(rwkv_mu[0]), ZR_PAD), w0=row(rwkv_w0[0]), a0=row(rwkv_a0[0]),
              wup=lora_rows(rwkv_w_up[0], 0), aup=lora_rows(rwkv_a_up[0], W_LORA),
              gup=lora_rows(rwkv_g_up[0], W_LORA + A_LORA),
              k_k=row(rwkv_k_k[0]), k_a=row(rwkv_k_a[0]), ln_w=row(rwkv_ln_w[0]), ln_b=row(rwkv_ln_b[0]),
              r_k3=rwkv_r_k[0][:, None, :])
    rp.update(mu_c=rp["mu"].T, w0_c=col(rwkv_w0[0]), a0_c=col(rwkv_a0[0]), k_k_c=col(rwkv_k_k[0]),
              k_a_c=col(rwkv_k_a[0]), wup_t=rp["wup"].T, aup_t=rp["aup"].T, gup_t=rp["gup"].T,
              r_k_c=head_col(rwkv_r_k[0]), ln_w_c=head_col(rwkv_ln_w[0]), ln_b_c=head_col(rwkv_ln_b[0]))
    fp = dict(w_out=w_out[0].astype(BF16), g_ffn=row(g_ffn_norm[0]), w_up=w_ffn_up[0].astype(BF16),
              conv_w=ffn_conv_w[0], conv_b=row(ffn_conv_b[0]), w_down=w_ffn_down[0].astype(BF16),
              g_final=row(g_final_norm))

    rows = nb * t_len
    xp = x_prompt.reshape(rows, D_MODEL)
    mod_p = _mod(jnp.pad(c_prompt, ((0, 8 - nb), (0, 0))), w_ada_bf, b_ada2).reshape(8, 1, 6 * D_MODEL)
    zr, qb, kb, vb, kt_p, vt_p, lft = _inproj_prompt(xp, mod_p, nb, t_len, g_attn, w_in_bf, bfp)
    cum = _cumsum(lft)
    o_r, st = _rwkv_prompt(zr, jnp.zeros((nb, 1, ZR_PAD), F32), nb, t_len, rp)
    o_f = _attn_prompt(qb, kb, vb, cum, nb, t_len)
    y_p, conv_p = _ffn_prompt(xp, o_r, o_f, mod_p, jnp.zeros((nb, CONV_W - 1, 2 * D_FF), F32), fp, nb, t_len)

    heads_t = lambda a: jnp.transpose(a.reshape(nb, N_HEADS, HEAD_DIM, t_len), (0, 3, 1, 2))[None]
    y_prompt = y_p.reshape(nb, t_len, D_MODEL)
    k_prompt = heads_t(kt_p)
    v_prompt = heads_t(vt_p)
    logf_prompt = jnp.transpose(lft, (0, 2, 1))[None]
    wkv_prompt = jnp.swapaxes(st, -1, -2).reshape(1, nb, N_HEADS, HEAD_DIM, HEAD_DIM)
    shift_prompt = zr.reshape(nb, t_len, ZR_PAD)[:, -1, :D_RWKV_IN][None]
    conv_prompt = conv_p[None]

    xs = x_sample.reshape(db, D_MODEL)
    mod_s = _mod(c_sample, w_ada_bf, b_ada2)
    zrt_s, qb_s, k_s, v_s, kt_s, vt_s, lft_s = _inproj_sample(xs, mod_s, g_attn, w_in_bf, bfp)
    shift_t = jnp.pad(state_shift[0].T, ((0, ZR_PAD - D_RWKV_IN), (0, 0)))
    vecs = _rwkv_prep(zrt_s, shift_t, rp)
    hd3 = lambda a: a.reshape(N_HEADS, HEAD_DIM, db)
    o_r_t, wkv_t = _rwkv_step([hd3(a) for a in vecs], rp, jnp.transpose(state_wkv[0], (1, 2, 3, 0)))
    kc = jnp.transpose(cache_k[0], (0, 2, 3, 1))
    vc = jnp.transpose(cache_v[0], (0, 2, 3, 1))
    lfc = jnp.transpose(cache_logf[0], (0, 2, 1))
    to3 = lambda a: a.reshape(db, N_HEADS, HEAD_DIM)
    o_f_s = _attn_sample(to3(qb_s), to3(k_s), to3(v_s), lft_s.T.reshape(db, 1, N_HEADS), kc, vc, lfc,
                         page_table.T)
    cp = state_ffn_conv[0]
    y_s, u_s = _ffn_sample(xs, o_r_t.reshape(D_GROUP, db), o_f_s.reshape(db, D_GROUP), mod_s,
                           cp[:, 0, :], cp[:, 1, :], fp)

    heads_s = lambda a: jnp.transpose(hd3(a), (2, 0, 1)).reshape(1, db, 1, N_HEADS, HEAD_DIM)
    y_sample = y_s.reshape(db, 1, D_MODEL)
    k_sample = heads_s(kt_s)
    v_sample = heads_s(vt_s)
    logf_sample = lft_s.T.reshape(1, db, 1, N_HEADS)
    wkv_sample = jnp.transpose(wkv_t, (3, 0, 1, 2))[None]
    shift_sample = zrt_s[:D_RWKV_IN].T[None]
    conv_sample = jnp.stack([cp[:, 1, :], u_s], axis=1)[None]
    return (y_prompt, y_sample, k_prompt, v_prompt, logf_prompt, wkv_prompt, shift_prompt, conv_prompt,
            k_sample, v_sample, logf_sample, wkv_sample, shift_sample, conv_sample)
```

```python
import functools

import jax
import jax.numpy as jnp
from jax import lax
from jax.experimental import pallas as pl
from jax.experimental.pallas import tpu as pltpu

F32 = jnp.float32
BF16 = jnp.bfloat16

D_MODEL = 1024
HEAD_DIM = 64
N_HEADS = 8
D_GROUP = N_HEADS * HEAD_DIM
W_LORA, A_LORA, G_LORA = 32, 32, 96
D_RWKV_IN = 3 * D_GROUP + W_LORA + A_LORA + G_LORA
ZR_PAD = 1792
LORA_OFF = 3 * D_GROUP
LORA_W = ZR_PAD - LORA_OFF
F_PAD = 128
W_IN_PAD = ZR_PAD + 3 * D_GROUP + F_PAD
D_FF = 2816
CONV_W = 3
PAGE = 128
RMS_EPS = 1e-6
GN_EPS = 64e-5
ATTN_SCALE = HEAD_DIM ** -0.5
LOG2E = 1.4426950408889634
LANES = 128
VMEM_LIMIT = 56 * 1024 * 1024

ROW_TILE = 256
RWKV_TILE = 256
CHUNK = 64
ATTN_TILE = 512
ATTN_KEYS = 2048
FF_CHUNK = D_FF
PAGES_PER_STEP = 16
PAGE_SLOTS = 3
FFN_TILE = 512


def _params(**kw):
    return pltpu.CompilerParams(vmem_limit_bytes=VMEM_LIMIT, **kw)


def _dot(a, b):
    return jnp.dot(a, b, preferred_element_type=F32)


def _dot_nt(a, b):
    return lax.dot_general(a, b, (((1,), (1,)), ((), ())), preferred_element_type=F32)


def _dot_tn(a, b):
    return lax.dot_general(a, b, (((0,), (0,)), ((), ())), preferred_element_type=F32)


def _bdot(a, b):
    return lax.dot_general(a, b, (((2,), (1,)), ((0,), (0,))), preferred_element_type=F32)


def _bdot_nt(a, b):
    return lax.dot_general(a, b, (((2,), (2,)), ((0,), (0,))), preferred_element_type=F32)


def _bdot_tn(a, b):
    return lax.dot_general(a, b, (((1,), (1,)), ((0,), (0,))), preferred_element_type=F32)


def _split3(x):
    hi = x.astype(BF16)
    r1 = x - hi.astype(F32)
    mid = r1.astype(BF16)
    lo = (r1 - mid.astype(F32)).astype(BF16)
    return hi, mid, lo


def _rms(x, g):
    return x * lax.rsqrt(jnp.mean(x * x, axis=-1, keepdims=True) + RMS_EPS) * g


def _mod_kernel(c_ref, w_ref, b_ref, o_ref):
    c = c_ref[...]
    s = (c * jax.nn.sigmoid(c)).astype(BF16)
    o_ref[...] = _dot(s, w_ref[...]) + b_ref[...]


def _mod(c, w_bf, b):
    m, n, tn = c.shape[0], w_bf.shape[1], 1536
    return pl.pallas_call(
        _mod_kernel, grid=(n // tn,),
        in_specs=[pl.BlockSpec((m, D_MODEL), lambda j: (0, 0)),
                  pl.BlockSpec((D_MODEL, tn), lambda j: (0, j)),
                  pl.BlockSpec((1, tn), lambda j: (0, j))],
        out_specs=pl.BlockSpec((m, tn), lambda j: (0, j)),
        out_shape=jax.ShapeDtypeStruct((m, n), F32),
        compiler_params=_params(), name="adaln_mod")(c, w_bf, b)


def _project(per_row, x_ref, sh_ref, sc_ref, g_ref, w_ref, bf_ref):
    sh = sh_ref[...] if per_row else sh_ref[0]
    sc = sc_ref[...] if per_row else sc_ref[0]
    h = (_rms(x_ref[...], g_ref[...]) * (1 + sc) + sh).astype(BF16)
    o = ZR_PAD
    zr = _dot(h, w_ref[:, 0:o])
    q = _dot(h, w_ref[:, o:o + D_GROUP]) * ATTN_SCALE
    k = _dot(h, w_ref[:, o + D_GROUP:o + 2 * D_GROUP])
    v = _dot(h, w_ref[:, o + 2 * D_GROUP:o + 3 * D_GROUP])
    f = _dot(h, w_ref[:, o + 3 * D_GROUP:o + 3 * D_GROUP + F_PAD])
    return zr, q, k, v, jax.nn.log_sigmoid(f + bf_ref[...])


def _inproj_prompt_kernel(x_ref, sh_ref, sc_ref, g_ref, w_ref, bf_ref,
                          zr_ref, qb_ref, kb_ref, vb_ref, kt_ref, vt_ref, lft_ref, k_s, v_s):
    zr, q, k, v, lf = _project(False, x_ref, sh_ref, sc_ref, g_ref, w_ref, bf_ref)
    zr_ref[...] = zr
    qb_ref[...] = (q * LOG2E).astype(BF16)
    kb_ref[...] = k.astype(BF16)
    vb_ref[...] = v.astype(BF16)
    k_s[...] = k
    v_s[...] = v
    kt_ref[0] = k_s[...].T
    vt_ref[0] = v_s[...].T
    lft_ref[0] = lf.T[0:N_HEADS, :]


def _inproj_sample_kernel(x_ref, sh_ref, sc_ref, g_ref, w_ref, bf_ref,
                          zrt_ref, qb_ref, k_ref, v_ref, kt_ref, vt_ref, lft_ref, zr_s):
    zr, q, k, v, lf = _project(True, x_ref, sh_ref, sc_ref, g_ref, w_ref, bf_ref)
    qb_ref[...] = q.astype(BF16)
    k_ref[...] = k
    v_ref[...] = v
    zr_s[...] = zr
    zrt_ref[...] = zr_s[...].T
    kt_ref[...] = k_ref[...].T
    vt_ref[...] = v_ref[...].T
    lft_ref[...] = lf.T[0:N_HEADS, :]


def _inproj_prompt(x2d, mod3, nb, t_len, g, w_bf, bfp):
    tm = ROW_TILE
    tpb = t_len // tm
    rows = nb * t_len
    mod_spec = lambda j: pl.BlockSpec((1, 1, D_MODEL), lambda i: (i // tpb, 0, j))
    row_spec = lambda n: pl.BlockSpec((tm, n), lambda i: (i, 0))
    col_spec = lambda n: pl.BlockSpec((1, n, tm), lambda i: (i // tpb, 0, i % tpb))
    const = lambda shape: pl.BlockSpec(shape, lambda i: (0, 0))
    return pl.pallas_call(
        _inproj_prompt_kernel, grid=(rows // tm,),
        in_specs=[row_spec(D_MODEL), mod_spec(0), mod_spec(1), const((1, D_MODEL)),
                  const((D_MODEL, W_IN_PAD)), const((1, F_PAD))],
        out_specs=[row_spec(ZR_PAD), row_spec(D_GROUP), row_spec(D_GROUP), row_spec(D_GROUP),
                   col_spec(D_GROUP), col_spec(D_GROUP), col_spec(N_HEADS)],
        out_shape=[jax.ShapeDtypeStruct((rows, ZR_PAD), F32),
                   jax.ShapeDtypeStruct((rows, D_GROUP), BF16),
                   jax.ShapeDtypeStruct((rows, D_GROUP), BF16),
                   jax.ShapeDtypeStruct((rows, D_GROUP), BF16),
                   jax.ShapeDtypeStruct((nb, D_GROUP, t_len), F32),
                   jax.ShapeDtypeStruct((nb, D_GROUP, t_len), F32),
                   jax.ShapeDtypeStruct((nb, N_HEADS, t_len), F32)],
        scratch_shapes=[pltpu.VMEM((tm, D_GROUP), F32), pltpu.VMEM((tm, D_GROUP), F32)],
        compiler_params=_params(dimension_semantics=("parallel",)), name="in_proj")(
            x2d, mod3, mod3, g, w_bf, bfp)


def _inproj_sample(x2d, mod, g, w_bf, bfp):
    m = x2d.shape[0]
    full = lambda r, c: pl.BlockSpec((r, c), lambda i: (0, 0))
    mod_spec = lambda j: pl.BlockSpec((m, D_MODEL), lambda i: (0, j))
    return pl.pallas_call(
        _inproj_sample_kernel, grid=(1,),
        in_specs=[full(m, D_MODEL), mod_spec(0), mod_spec(1), full(1, D_MODEL), full(D_MODEL, W_IN_PAD),
                  full(1, F_PAD)],
        out_specs=[full(ZR_PAD, m), full(m, D_GROUP), full(m, D_GROUP), full(m, D_GROUP),
                   full(D_GROUP, m), full(D_GROUP, m), full(N_HEADS, m)],
        out_shape=[jax.ShapeDtypeStruct((ZR_PAD, m), F32),
                   jax.ShapeDtypeStruct((m, D_GROUP), BF16),
                   jax.ShapeDtypeStruct((m, D_GROUP), F32),
                   jax.ShapeDtypeStruct((m, D_GROUP), F32),
                   jax.ShapeDtypeStruct((D_GROUP, m), F32),
                   jax.ShapeDtypeStruct((D_GROUP, m), F32),
                   jax.ShapeDtypeStruct((N_HEADS, m), F32)],
        scratch_shapes=[pltpu.VMEM((m, ZR_PAD), F32)],
        compiler_params=_params(), name="in_proj_sample")(x2d, mod, mod, g, w_bf, bfp)


def _cumsum_kernel(x_ref, o_ref):
    nb, _, t_len = x_ref.shape
    blk = 256
    r = lax.broadcasted_iota(jnp.int32, (blk, blk), 0)
    c = lax.broadcasted_iota(jnp.int32, (blk, blk), 1)
    upper = (r <= c).astype(BF16)
    for b in range(nb):
        carry = jnp.zeros((N_HEADS, 1), F32)
        for j in range(t_len // blk):
            sl = slice(j * blk, (j + 1) * blk)
            hi, mid, lo = _split3(x_ref[b, :, sl])
            cs = (_dot(lo, upper) + _dot(mid, upper)) + _dot(hi, upper) + carry
            o_ref[b, :, sl] = cs
            carry = cs[:, blk - 1:blk]


def _cumsum(lft):
    return pl.pallas_call(
        _cumsum_kernel, out_shape=jax.ShapeDtypeStruct(lft.shape, F32),
        compiler_params=_params(), name="logf_cumsum")(lft)


def _rwkv_mix(zm, w0, wup, a0, aup, gup, k_k, k_a):
    r = zm[:, 0:D_GROUP]
    k = zm[:, D_GROUP:2 * D_GROUP]
    v = zm[:, 2 * D_GROUP:3 * D_GROUP]
    lora = zm[:, LORA_OFF:ZR_PAD]
    w = w0 + _dot(jnp.tanh(lora).astype(BF16), wup)
    logd = -jnp.exp(-jax.nn.softplus(-w) - 0.5)
    a = jax.nn.sigmoid(a0 + _dot(lora.astype(BF16), aup))
    g = _dot(jax.nn.sigmoid(lora).astype(BF16), gup)
    kk = k * k_k
    k = k * (1 + (a - 1) * k_a)
    return r, k, v, logd, kk, a, g


def _group_norm(y):
    mean = jnp.mean(y, axis=-1, keepdims=True)
    var = jnp.mean(jnp.square(y - mean), axis=-1, keepdims=True)
    return (y - mean) * lax.rsqrt(var + GN_EPS)


def _rwkv_kernel(z_ref, sh0_ref, mu_ref, w0_ref, wup_ref, a0_ref, aup_ref, gup_ref, kk_ref, ka_ref,
                 rk_ref, lnw_ref, lnb_ref, o_ref, st_ref, prev_s, st_s, hm_s, yn_s, bon_s):
    tt, c = RWKV_TILE, CHUNK
    t = pl.program_id(1)

    @pl.when(t == 0)
    def _():
        prev_s[0:1, :] = sh0_ref[0]
        st_s[...] = jnp.zeros_like(st_s)

    z = z_ref[...]
    row = lax.broadcasted_iota(jnp.int32, (tt, 1), 0)
    zprev = jnp.where(row == 0, prev_s[0:1, :], pltpu.roll(z, 1, axis=0))
    prev_s[0:1, :] = z[tt - 1:tt, :]
    zm = z + mu_ref[...] * (zprev - z)
    r, k, v, logd, kk, a, g = _rwkv_mix(zm, w0_ref[...], wup_ref[...], a0_ref[...], aup_ref[...],
                                        gup_ref[...], kk_ref[...], ka_ref[...])
    ri = lax.broadcasted_iota(jnp.int32, (tt, tt), 0)
    ci = lax.broadcasted_iota(jnp.int32, (tt, tt), 1)
    tri = ((ri // c == ci // c) & (ci <= ri)).astype(BF16)
    hi, mid, lo = _split3(logd)
    cl = (_dot(tri, lo) + _dot(tri, mid)) + _dot(tri, hi)
    for h in range(N_HEADS):
        sl = slice(h * HEAD_DIM, (h + 1) * HEAD_DIM)
        for n, val in enumerate((r, k, v, kk, a, logd, cl)):
            hm_s[n, h] = val[:, sl]

    rr = lax.broadcasted_iota(jnp.int32, (c, c), 0)
    cc = lax.broadcasted_iota(jnp.int32, (c, c), 1)
    strict = (cc < rr).astype(F32)
    eye = rr == cc
    r2 = lax.broadcasted_iota(jnp.int32, (c, 2 * c), 0)
    c2 = lax.broadcasted_iota(jnp.int32, (c, 2 * c), 1)
    sgn = jnp.where(c2 < c, jnp.where(c2 <= r2, 1.0, 0.0), jnp.where(c2 - c <= r2, -1.0, 0.0)).astype(F32)
    nch = tt // c
    grp = N_HEADS * nch
    r_, k_, v_, kk_, a_, ld_, cl_ = (hm_s[n].reshape(grp, c, HEAD_DIM) for n in range(7))
    nrm = jnp.sqrt(jnp.sum(kk_ * kk_, axis=-1, keepdims=True))
    kkn = kk_ / jnp.maximum(nrm, 1e-12)
    b_ = kkn * a_
    cl_end = cl_[:, c - 1:c, :]
    e_end = jnp.exp(cl_end - cl_)
    g_inv = jnp.exp(-cl_)
    kkg = kkn * jnp.exp(cl_ - ld_)
    rg = r_ * jnp.exp(cl_)
    kd = (k_ * g_inv).astype(BF16)
    bd = (b_ * g_inv).astype(BF16)
    kkg_b = kkg.astype(BF16)
    a_k = _bdot_nt(kkg_b, kd) * strict
    a_b = _bdot_nt(kkg_b, bd) * strict
    ll = _bdot_nt(rg.astype(BF16), jnp.concatenate([kd, bd], axis=1)) * sgn
    akv = _bdot(a_k.astype(BF16), v_.astype(BF16))
    x = jnp.concatenate([akv, kkg], axis=2)
    p = (-a_b).astype(BF16)
    x = x + _bdot(p, x.astype(BF16))
    n = 2
    while n < c:
        p = _bdot(p, p).astype(BF16)
        x = x + _bdot(p, x.astype(BF16))
        n *= 2
    rhs = jnp.concatenate([jnp.concatenate([v_, jnp.zeros_like(v_)], axis=2), x], axis=1).astype(BF16)
    yq = _bdot(ll.astype(BF16), rhs)
    kb = jnp.concatenate([k_ * e_end, -(b_ * e_end)], axis=1).astype(BF16)
    sw = _bdot_tn(kb, rhs)
    qc = (rg + yq[:, :, HEAD_DIM:]).astype(BF16)
    m = (jnp.where(eye, jnp.broadcast_to(jnp.exp(cl_end), (grp, c, c)), 0.0) + sw[:, :, HEAD_DIM:]).astype(BF16)
    by_head = lambda t: t.reshape(N_HEADS, nch, *t.shape[1:])
    qc, m, y0, w = by_head(qc), by_head(m), by_head(yq[:, :, 0:HEAD_DIM]), by_head(sw[:, :, 0:HEAD_DIM])
    st = st_s[...]
    ys = []
    for i in range(nch):
        st_b = st.astype(BF16)
        ys.append(_bdot(qc[:, i], st_b) + y0[:, i])
        st = _bdot(m[:, i], st_b) + w[:, i]
    st_s[...] = st
    yn_s[...] = _group_norm(jnp.concatenate(ys, axis=1))
    bon_s[...] = jnp.sum(hm_s[0] * hm_s[1] * rk_ref[...], axis=-1, keepdims=True) * hm_s[2]
    yn = jnp.concatenate([yn_s[h] for h in range(N_HEADS)], axis=1)
    bon = jnp.concatenate([bon_s[h] for h in range(N_HEADS)], axis=1)
    o_ref[...] = ((yn * lnw_ref[...] + lnb_ref[...] + bon) * g).astype(BF16)
    st_ref[...] = st_s[...]


def _rwkv_prompt(zr, shift0, nb, t_len, rp):
    tt = RWKV_TILE
    nt = t_len // tt
    const = lambda shape: pl.BlockSpec(shape, lambda b, t: (0,) * len(shape))
    return pl.pallas_call(
        _rwkv_kernel, grid=(nb, nt),
        in_specs=[pl.BlockSpec((tt, ZR_PAD), lambda b, t: (b * nt + t, 0)),
                  pl.BlockSpec((1, 1, ZR_PAD), lambda b, t: (b, 0, 0)),
                  const((1, ZR_PAD)), const((1, D_GROUP)), const((LORA_W, D_GROUP)),
                  const((1, D_GROUP)), const((LORA_W, D_GROUP)), const((LORA_W, D_GROUP)),
                  const((1, D_GROUP)), const((1, D_GROUP)), const((N_HEADS, 1, HEAD_DIM)),
                  const((1, D_GROUP)), const((1, D_GROUP))],
        out_specs=[pl.BlockSpec((tt, D_GROUP), lambda b, t: (b * nt + t, 0)),
                   pl.BlockSpec((N_HEADS, HEAD_DIM, HEAD_DIM), lambda b, t: (b, 0, 0))],
        out_shape=[jax.ShapeDtypeStruct((nb * t_len, D_GROUP), BF16),
                   jax.ShapeDtypeStruct((nb * N_HEADS, HEAD_DIM, HEAD_DIM), F32)],
        scratch_shapes=[pltpu.VMEM((8, ZR_PAD), F32),
                        pltpu.VMEM((N_HEADS, HEAD_DIM, HEAD_DIM), F32),
                        pltpu.VMEM((7, N_HEADS, tt, HEAD_DIM), F32),
                        pltpu.VMEM((N_HEADS, tt, HEAD_DIM), F32),
                        pltpu.VMEM((N_HEADS, tt, HEAD_DIM), F32)],
        compiler_params=_params(dimension_semantics=("arbitrary", "arbitrary")), name="rwkv_scan")(
            zr, shift0, rp["mu"], rp["w0"], rp["wup"], rp["a0"], rp["aup"], rp["gup"], rp["k_k"],
            rp["k_a"], rp["r_k3"], rp["ln_w"], rp["ln_b"])


def _attn_kernel(q_ref, k_ref, v_ref, c_ref, o_ref):
    tq, tk = ATTN_TILE, ATTN_KEYS
    qi = pl.program_id(2)
    q_start = qi * tq
    q2 = q_ref[...]
    lane = lax.broadcasted_iota(jnp.int32, (1, LANES), 1)
    lo = lane < HEAD_DIM
    zero = jnp.zeros_like(q2)
    q_h = (jnp.where(lo, q2, zero), jnp.where(lo, zero, q2))
    c_ref0 = c_ref[0, 0, :, pl.ds(pl.multiple_of(q_start, tq), LANES)][:, 0:1]

    def step(off, width, carry, masked):
        kb = k_ref[pl.ds(off, width), :]
        vb = v_ref[pl.ds(off, width), :]
        ck = c_ref[0, 0, :, pl.ds(off, width)]
        if masked:
            row = lax.broadcasted_iota(jnp.int32, (tq, width), 0)
            col = lax.broadcasted_iota(jnp.int32, (tq, width), 1)
            causal = col + off <= row + q_start
        out = []
        for hd in range(2):
            m, l, acc = carry[3 * hd:3 * hd + 3]
            s = _dot_nt(q_h[hd], kb) + (c_ref0[hd:hd + 1, :] - ck[hd:hd + 1, :]) * LOG2E
            if masked:
                s = jnp.where(causal, s, -jnp.inf)
            m_new = jnp.maximum(m, jnp.max(s, axis=-1, keepdims=True))
            alpha = jnp.exp2(m - m_new)
            p = jnp.exp2(s - m_new)
            l = alpha * l + jnp.sum(p, axis=-1, keepdims=True)
            acc = alpha * acc + _dot(p.astype(BF16), vb)
            out += [m_new, l, acc]
        return tuple(out)

    init = (jnp.full((tq, 1), -jnp.inf, F32), jnp.zeros((tq, 1), F32), jnp.zeros((tq, LANES), F32)) * 2
    n_full = q_start // tk
    carry = lax.fori_loop(0, n_full, lambda j, cr: step(pl.multiple_of(j * tk, tk), tk, cr, False), init)
    tail = pl.multiple_of(n_full * tk, tk)
    pieces = (q_start + tq - tail) // tq
    carry = lax.switch(pieces - 1, [functools.partial(step, tail, (n + 1) * tq, masked=True)
                                    for n in range(tk // tq)], carry)
    _, l0, acc0, _, l1, acc1 = carry
    o_ref[...] = jnp.where(lo, acc0 / l0, acc1 / l1).astype(BF16)


def _attn_prompt(qb, kb, vb, cum, nb, t_len):
    tq = ATTN_TILE
    nq = t_len // tq
    n_pairs = D_GROUP // LANES
    cum4 = cum.reshape(nb, n_pairs, 2, t_len)
    return pl.pallas_call(
        _attn_kernel, grid=(nb, n_pairs, nq),
        in_specs=[pl.BlockSpec((tq, LANES), lambda b, hp, qi: (b * nq + qi, hp)),
                  pl.BlockSpec((t_len, LANES), lambda b, hp, qi: (b, hp)),
                  pl.BlockSpec((t_len, LANES), lambda b, hp, qi: (b, hp)),
                  pl.BlockSpec((1, 1, 2, t_len), lambda b, hp, qi: (b, hp, 0, 0))],
        out_specs=pl.BlockSpec((tq, LANES), lambda b, hp, qi: (b * nq + qi, hp)),
        out_shape=jax.ShapeDtypeStruct((nb * t_len, D_GROUP), BF16),
        compiler_params=_params(dimension_semantics=("parallel", "parallel", "arbitrary")),
        name="fox_prompt")(qb, kb, vb, cum4)


def _ffn_front(x, o_r, o_f, gta, shf, scf, wout_ref, gffn):
    attn = _dot(o_r, wout_ref[0:D_GROUP, :]) + _dot(o_f, wout_ref[D_GROUP:2 * D_GROUP, :])
    x1 = x + gta * attn
    return x1, (_rms(x1, gffn) * (1 + scf) + shf).astype(BF16)


def _ffn_prompt_kernel(tpb, x_ref, or_ref, of_ref, gta_ref, shf_ref, scf_ref, gtf_ref, wout_ref, gffn_ref,
                       wup_ref, cw_ref, cb_ref, wdn_ref, gfin_ref, cp_ref, y_ref, conv_ref, prev_s):
    i = pl.program_id(0)

    @pl.when(i % tpb == 0)
    def _():
        prev_s[0:2, :] = cp_ref[0]

    x1, h2 = _ffn_front(x_ref[...], or_ref[...], of_ref[...], gta_ref[0], shf_ref[0], scf_ref[0],
                        wout_ref, gffn_ref[...])
    tm = x1.shape[0]
    row = lax.broadcasted_iota(jnp.int32, (tm, 1), 0)

    def conv(cols):
        u = _dot(h2, wup_ref[:, cols])
        p0 = prev_s[0:1, cols]
        p1 = prev_s[1:2, cols]
        u1 = jnp.where(row == 0, p1, pltpu.roll(u, 1, axis=0))
        u2 = jnp.where(row == 0, p0, jnp.where(row == 1, p1, pltpu.roll(u, 2, axis=0)))
        prev_s[0:2, cols] = u[tm - 2:tm, :]
        return cb_ref[:, cols] + cw_ref[0:1, cols] * u2 + cw_ref[1:2, cols] * u1 + cw_ref[2:3, cols] * u

    acc = jnp.zeros((tm, D_MODEL), F32)
    for j in range(D_FF // FF_CHUNK):
        gate = conv(slice(j * FF_CHUNK, (j + 1) * FF_CHUNK))
        val = conv(slice(D_FF + j * FF_CHUNK, D_FF + (j + 1) * FF_CHUNK))
        act = (gate * jax.nn.sigmoid(gate) * val).astype(BF16)
        acc = acc + _dot(act, wdn_ref[j * FF_CHUNK:(j + 1) * FF_CHUNK, :])
    y_ref[...] = _rms(x1 + gtf_ref[0] * acc, gfin_ref[...])
    conv_ref[0] = prev_s[0:2, :]


def _ffn_prompt(x2d, o_r, o_f, mod3, conv0, fp, nb, t_len):
    tm = FFN_TILE
    tpb = t_len // tm
    rows = nb * t_len
    row_spec = lambda n: pl.BlockSpec((tm, n), lambda i: (i, 0))
    mod_spec = lambda j: pl.BlockSpec((1, 1, D_MODEL), lambda i: (i // tpb, 0, j))
    const = lambda shape: pl.BlockSpec(shape, lambda i: (0, 0))
    weight = lambda shape: pl.BlockSpec(shape, lambda i: (0, 0), pipeline_mode=pl.Buffered(1))
    conv_spec = pl.BlockSpec((1, CONV_W - 1, 2 * D_FF), lambda i: (i // tpb, 0, 0))
    return pl.pallas_call(
        functools.partial(_ffn_prompt_kernel, tpb), grid=(rows // tm,),
        in_specs=[row_spec(D_MODEL), row_spec(D_GROUP), row_spec(D_GROUP),
                  mod_spec(2), mod_spec(3), mod_spec(4), mod_spec(5),
                  weight((D_MODEL, D_MODEL)), const((1, D_MODEL)), weight((D_MODEL, 2 * D_FF)),
                  const((CONV_W, 2 * D_FF)), const((1, 2 * D_FF)), weight((D_FF, D_MODEL)),
                  const((1, D_MODEL)), conv_spec],
        out_specs=[row_spec(D_MODEL), conv_spec],
        out_shape=[jax.ShapeDtypeStruct((rows, D_MODEL), F32),
                   jax.ShapeDtypeStruct((nb, CONV_W - 1, 2 * D_FF), F32)],
        scratch_shapes=[pltpu.VMEM((8, 2 * D_FF), F32)],
        compiler_params=_params(dimension_semantics=("arbitrary",)), name="ffn_prompt")(
            x2d, o_r, o_f, mod3, mod3, mod3, mod3, fp["w_out"], fp["g_ffn"], fp["w_up"], fp["conv_w"],
            fp["conv_b"], fp["w_down"], fp["g_final"], conv0)


def _ffn_sample_kernel(x_ref, ort_ref, of_ref, gta_ref, shf_ref, scf_ref, gtf_ref, wout_ref, gffn_ref,
                       wup_ref, cw_ref, cb_ref, wdn_ref, gfin_ref, cp0_ref, cp1_ref, y_ref, u_ref):
    attn = (_dot(ort_ref[...].T.astype(BF16), wout_ref[0:D_GROUP, :])
            + _dot(of_ref[...].astype(BF16), wout_ref[D_GROUP:2 * D_GROUP, :]))
    x1 = x_ref[...] + gta_ref[...] * attn
    h2 = (_rms(x1, gffn_ref[...]) * (1 + scf_ref[...]) + shf_ref[...]).astype(BF16)

    def conv(cols):
        u = _dot(h2, wup_ref[:, cols])
        u_ref[:, cols] = u
        return (cb_ref[:, cols] + cw_ref[0:1, cols] * cp0_ref[:, cols] + cw_ref[1:2, cols] * cp1_ref[:, cols]
                + cw_ref[2:3, cols] * u)

    acc = jnp.zeros(x1.shape, F32)
    for j in range(D_FF // FF_CHUNK):
        gate = conv(slice(j * FF_CHUNK, (j + 1) * FF_CHUNK))
        val = conv(slice(D_FF + j * FF_CHUNK, D_FF + (j + 1) * FF_CHUNK))
        act = (gate * jax.nn.sigmoid(gate) * val).astype(BF16)
        acc = acc + _dot(act, wdn_ref[j * FF_CHUNK:(j + 1) * FF_CHUNK, :])
    y_ref[...] = _rms(x1 + gtf_ref[...] * acc, gfin_ref[...])


def _ffn_sample(x2d, o_r_t, o_f, mod, cp0, cp1, fp):
    m = x2d.shape[0]
    full = lambda n: pl.BlockSpec((m, n), lambda i: (0, 0))
    mod_spec = lambda j: pl.BlockSpec((m, D_MODEL), lambda i: (0, j))
    const = lambda shape: pl.BlockSpec(shape, lambda i: (0, 0))
    return pl.pallas_call(
        _ffn_sample_kernel, grid=(1,),
        in_specs=[full(D_MODEL), const((D_GROUP, m)), full(D_GROUP),
                  mod_spec(2), mod_spec(3), mod_spec(4), mod_spec(5),
                  const((D_MODEL, D_MODEL)), const((1, D_MODEL)), const((D_MODEL, 2 * D_FF)),
                  const((CONV_W, 2 * D_FF)), const((1, 2 * D_FF)), const((D_FF, D_MODEL)),
                  const((1, D_MODEL)), full(2 * D_FF), full(2 * D_FF)],
        out_specs=[full(D_MODEL), full(2 * D_FF)],
        out_shape=[jax.ShapeDtypeStruct((m, D_MODEL), F32), jax.ShapeDtypeStruct((m, 2 * D_FF), F32)],
        compiler_params=_params(), name="ffn_sample")(
            x2d, o_r_t, o_f, mod, mod, mod, mod, fp["w_out"], fp["g_ffn"], fp["w_up"], fp["conv_w"],
            fp["conv_b"], fp["w_down"], fp["g_final"], cp0, cp1)


def _rwkv_prep_kernel(zt_ref, sht_ref, mu_ref, w0_ref, wupt_ref, a0_ref, aupt_ref, gupt_ref, kk_ref, ka_ref,
                      r_ref, k_ref, v_ref, d_ref, kko_ref, a_ref, g_ref):
    z = zt_ref[...]
    zm = z + mu_ref[...] * (sht_ref[...] - z)
    k = zm[D_GROUP:2 * D_GROUP, :]
    lora = zm[LORA_OFF:ZR_PAD, :]
    w = w0_ref[...] + _dot(wupt_ref[...], jnp.tanh(lora).astype(BF16))
    a = jax.nn.sigmoid(a0_ref[...] + _dot(aupt_ref[...], lora.astype(BF16)))
    r_ref[...] = zm[0:D_GROUP, :]
    k_ref[...] = k * (1 + (a - 1) * ka_ref[...])
    v_ref[...] = zm[2 * D_GROUP:3 * D_GROUP, :]
    d_ref[...] = jnp.exp(-jnp.exp(-jax.nn.softplus(-w) - 0.5))
    kko_ref[...] = k * kk_ref[...]
    a_ref[...] = a
    g_ref[...] = _dot(gupt_ref[...], jax.nn.sigmoid(lora).astype(BF16))


def _rwkv_prep(zrt, shift_t, rp):
    m = zrt.shape[1]
    return pl.pallas_call(
        _rwkv_prep_kernel,
        out_shape=[jax.ShapeDtypeStruct((D_GROUP, m), F32)] * 7,
        compiler_params=_params(), name="rwkv_prep")(
            zrt, shift_t, rp["mu_c"], rp["w0_c"], rp["wup_t"], rp["a0_c"], rp["aup_t"], rp["gup_t"],
            rp["k_k_c"], rp["k_a_c"])


def _rwkv_step_kernel(r_ref, k_ref, v_ref, d_ref, kk_ref, a_ref, g_ref, rk_ref, lnw_ref, lnb_ref, s_ref,
                      o_ref, so_ref, y_s):
    r, k, d, kk, a = (ref[0] for ref in (r_ref, k_ref, d_ref, kk_ref, a_ref))
    nrm = jnp.sqrt(jnp.sum(kk * kk, axis=0, keepdims=True))
    kkn = kk / jnp.maximum(nrm, 1e-12)
    bb = kkn * a

    def body(i, carry):
        s = s_ref[0, i]
        sa = -jnp.sum(s * kkn, axis=0, keepdims=True)
        s = s * d + sa * bb + v_ref[0, pl.ds(i, 1), :] * k
        so_ref[0, i] = s
        y_s[pl.ds(i, 1), :] = jnp.sum(s * r, axis=0, keepdims=True)
        return carry

    lax.fori_loop(0, HEAD_DIM, body, 0)
    y = y_s[...]
    mean = jnp.mean(y, axis=0, keepdims=True)
    var = jnp.mean(jnp.square(y - mean), axis=0, keepdims=True)
    yn = (y - mean) * lax.rsqrt(var + GN_EPS)
    bonus = jnp.sum(r * k * rk_ref[0], axis=0, keepdims=True) * v_ref[0]
    o_ref[0] = (yn * lnw_ref[0] + lnb_ref[0] + bonus) * g_ref[0]


def _rwkv_step(vecs, rp, state_t):
    m = state_t.shape[-1]
    vec_spec = pl.BlockSpec((1, HEAD_DIM, m), lambda h: (h, 0, 0))
    par_spec = pl.BlockSpec((1, HEAD_DIM, 1), lambda h: (h, 0, 0))
    st_spec = pl.BlockSpec((1, HEAD_DIM, HEAD_DIM, m), lambda h: (h, 0, 0, 0))
    return pl.pallas_call(
        _rwkv_step_kernel, grid=(N_HEADS,),
        in_specs=[vec_spec] * 7 + [par_spec] * 3 + [st_spec],
        out_specs=[vec_spec, st_spec],
        out_shape=[jax.ShapeDtypeStruct((N_HEADS, HEAD_DIM, m), F32),
                   jax.ShapeDtypeStruct(state_t.shape, F32)],
        scratch_shapes=[pltpu.VMEM((HEAD_DIM, m), F32)],
        compiler_params=_params(dimension_semantics=("parallel",)), name="rwkv_step")(
            *vecs, rp["r_k_c"], rp["ln_w_c"], rp["ln_b_c"], state_t)


def _lane_prefix(x):
    lane = lax.broadcasted_iota(jnp.int32, x.shape, 1)
    s = 1
    while s < x.shape[1]:
        x = x + jnp.where(lane >= s, pltpu.roll(x, s, axis=1), 0.0)
        s *= 2
    return x


def _pattn_kernel(npg, pt_ref, q_ref, kn_ref, vn_ref, lfn_ref, *refs):
    pg = PAGES_PER_STEP
    kc_ref, vc_ref, lfc_ref, o_ref, kbuf, vbuf, lfbuf, sem, m_s, l_s, c_s, qc_s, acc_s = refs
    g = pl.program_id(1)
    n_seq = pl.num_programs(0)
    step = pl.program_id(0) * npg + g
    last = n_seq * npg - 1
    slot = step % PAGE_SLOTS
    ahead = PAGE_SLOTS - 1

    def page_copies(st, sl, fetch):
        seq, grp = st // npg, st % npg
        out = []
        for p in range(pg):
            page = pt_ref[grp * pg + p, seq] if fetch else 0
            out.append(pltpu.make_async_copy(kc_ref.at[page], kbuf.at[sl, p], sem.at[sl, 0]))
            out.append(pltpu.make_async_copy(vc_ref.at[page], vbuf.at[sl, p], sem.at[sl, 1]))
            out.append(pltpu.make_async_copy(lfc_ref.at[page], lfbuf.at[sl, p], sem.at[sl, 2]))
        return out

    @pl.when(step == 0)
    def _():
        for t in range(ahead):
            for cp in page_copies(jnp.minimum(t, last), t, True):
                cp.start()

    for cp in page_copies(step, slot, False):
        cp.wait()
    for cp in page_copies(jnp.minimum(step + ahead, last), (step + ahead) % PAGE_SLOTS, True):
        cp.start()

    n = HEAD_DIM
    eye = lax.broadcasted_iota(jnp.int32, (n, n), 0) == lax.broadcasted_iota(jnp.int32, (n, n), 1)

    @pl.when(g == 0)
    def _():
        m_s[...] = jnp.full_like(m_s, -jnp.inf)
        l_s[...] = jnp.zeros_like(l_s)
        c_s[...] = jnp.zeros_like(c_s)
        acc_s[...] = jnp.zeros_like(acc_s)
        q = q_ref[0].astype(F32)
        for h in range(N_HEADS):
            col = jnp.sum(jnp.where(eye, q[h:h + 1, :], 0.0), axis=-1, keepdims=True)
            qc_s[h] = jnp.broadcast_to(col, (n, PAGE))

    within = _lane_prefix(lfbuf[slot].reshape(pg * N_HEADS, PAGE))
    run = c_s[:, 0:1]
    scores = []
    for p in range(pg):
        w_p = within[p * N_HEADS:(p + 1) * N_HEADS, :]
        rows = [jnp.sum(qc_s[h] * kbuf[slot, p, h], axis=0, keepdims=True) for h in range(N_HEADS)]
        scores.append(jnp.concatenate(rows, axis=0) - (run + w_p))
        run = run + w_p[:, PAGE - 1:PAGE]
    c_s[...] = jnp.broadcast_to(run, c_s.shape)
    m_old = m_s[:, 0:1]
    m_new = m_old
    for s in scores:
        m_new = jnp.maximum(m_new, jnp.max(s, axis=-1, keepdims=True))
    alpha = jnp.exp(m_old - m_new)
    probs = [jnp.exp(s - m_new) for s in scores]
    l = alpha * l_s[:, 0:1]
    for pr in probs:
        l = l + jnp.sum(pr, axis=-1, keepdims=True)
    m_s[...] = jnp.broadcast_to(m_new, m_s.shape)
    l_s[...] = jnp.broadcast_to(l, l_s.shape)
    for h in range(N_HEADS):
        a = acc_s[h] * alpha[h:h + 1, :]
        for p in range(pg):
            a = a + probs[p][h:h + 1, :] * vbuf[slot, p, h]
        acc_s[h] = a

    @pl.when(step == last)
    def _():
        for t in range(1, PAGE_SLOTS):
            for cp in page_copies(step, (step + t) % PAGE_SLOTS, False):
                cp.wait()

    @pl.when(g == npg - 1)
    def _():
        rows = []
        for h in range(N_HEADS):
            col = jnp.sum(acc_s[h], axis=-1, keepdims=True)
            rows.append(jnp.sum(jnp.where(eye, col, 0.0), axis=0, keepdims=True))
        o_past = jnp.concatenate(rows, axis=0)
        r8 = lax.broadcasted_iota(jnp.int32, (N_HEADS, N_HEADS), 0)
        c8 = lax.broadcasted_iota(jnp.int32, (N_HEADS, N_HEADS), 1)
        lfn_col = jnp.sum(jnp.where(r8 == c8, lfn_ref[0], 0.0), axis=-1, keepdims=True)
        s_new = jnp.sum(q_ref[0].astype(F32) * kn_ref[0], axis=-1, keepdims=True) - (run + lfn_col)
        m_fin = jnp.maximum(m_new, s_new)
        beta = jnp.exp(m_new - m_fin)
        p_new = jnp.exp(s_new - m_fin)
        o_ref[0] = (o_past * beta + p_new * vn_ref[0]) / (l * beta + p_new)


def _attn_sample(qb3, kn3, vn3, lfn3, kc, vc, lfc, pt_t):
    n_pages, nb = pt_t.shape
    pg = PAGES_PER_STEP
    npg = n_pages // pg
    vec_spec = pl.BlockSpec((1, N_HEADS, HEAD_DIM), lambda b, g, pt: (b, 0, 0))
    hbm = pl.BlockSpec(memory_space=pl.ANY)
    grid_spec = pltpu.PrefetchScalarGridSpec(
        num_scalar_prefetch=1, grid=(nb, npg),
        in_specs=[vec_spec, vec_spec, vec_spec, pl.BlockSpec((1, 1, N_HEADS), lambda b, g, pt: (b, 0, 0)),
                  hbm, hbm, hbm],
        out_specs=vec_spec,
        scratch_shapes=[pltpu.VMEM((PAGE_SLOTS, pg, N_HEADS, HEAD_DIM, PAGE), F32),
                        pltpu.VMEM((PAGE_SLOTS, pg, N_HEADS, HEAD_DIM, PAGE), F32),
                        pltpu.VMEM((PAGE_SLOTS, pg, N_HEADS, PAGE), F32),
                        pltpu.SemaphoreType.DMA((PAGE_SLOTS, 3)),
                        pltpu.VMEM((N_HEADS, LANES), F32), pltpu.VMEM((N_HEADS, LANES), F32),
                        pltpu.VMEM((N_HEADS, LANES), F32),
                        pltpu.VMEM((N_HEADS, HEAD_DIM, PAGE), F32), pltpu.VMEM((N_HEADS, HEAD_DIM, PAGE), F32)])
    return pl.pallas_call(
        functools.partial(_pattn_kernel, npg), grid_spec=grid_spec,
        out_shape=jax.ShapeDtypeStruct((nb, N_HEADS, HEAD_DIM), F32),
        compiler_params=_params(dimension_semantics=("arbitrary", "arbitrary")), name="fox_sample")(
            pt_t, qb3, kn3, vn3, lfn3, kc, vc, lfc)


def _pad_cols(a, n):
    return jnp.pad(a, ((0, 0), (0, n - a.shape[1])))


def kernel(x_prompt, x_sample, c_prompt, c_sample, cache_k, cache_v, cache_logf, page_table, state_wkv,
           state_shift, state_ffn_conv, w_ada, b_ada, g_attn_norm, w_in, b_forget, rwkv_mu, rwkv_w0,
           rwkv_w_up, rwkv_a0, rwkv_a_up, rwkv_g_up, rwkv_k_k, rwkv_k_a, rwkv_r_k, rwkv_ln_w, rwkv_ln_b,
           w_out, g_ffn_norm, w_ffn_up, ffn_conv_w, ffn_conv_b, w_ffn_down, g_final_norm):
    depth = w_in.shape[0]
    assert depth == 1, "one layer per call"
    nb, t_len, _ = x_prompt.shape
    db, dt, _ = x_sample.shape
    assert dt == 1, "the sample group decodes one token per sequence"
    assert t_len % FFN_TILE == 0 and t_len % ATTN_TILE == 0 and ATTN_KEYS % ATTN_TILE == 0 and db % 8 == 0

    w_ada_bf = w_ada[0].astype(BF16)
    b_ada2 = b_ada[0][None, :]
    wi = w_in[0]
    o = D_RWKV_IN
    w_in_bf = jnp.concatenate(
        [_pad_cols(wi[:, :o], ZR_PAD), wi[:, o:o + 3 * D_GROUP], _pad_cols(wi[:, o + 3 * D_GROUP:], F_PAD)],
        axis=1).astype(BF16)
    bfp = _pad_cols(b_forget[0][None, :], F_PAD)
    g_attn = g_attn_norm[0][None, :]
    lora_rows = lambda w, off: jnp.pad(w, ((off, LORA_W - off - w.shape[0]), (0, 0))).astype(BF16)
    row = lambda a: a[None, :]
    col = lambda a: a[:, None]
    head_col = lambda a: a.reshape(N_HEADS, HEAD_DIM, 1)
    rp = dict(mu=_pad_cols(row(rwkv_mu[0]), ZR_PAD), w0=row(rwkv_w0[0]), a0=row(rwkv_a0[0]),
              wup=lora_rows(rwkv_w_up[0], 0), aup=lora_rows(rwkv_a_up[0], W_LORA),
              gup=lora_rows(rwkv_g_up[0], W_LORA + A_LORA),
              k_k=row(rwkv_k_k[0]), k_a=row(rwkv_k_a[0]), ln_w=row(rwkv_ln_w[0]), ln_b=row(rwkv_ln_b[0]),
              r_k3=rwkv_r_k[0][:, None, :])
    rp.update(mu_c=rp["mu"].T, w0_c=col(rwkv_w0[0]), a0_c=col(rwkv_a0[0]), k_k_c=col(rwkv_k_k[0]),
              k_a_c=col(rwkv_k_a[0]), wup_t=rp["wup"].T, aup_t=rp["aup"].T, gup_t=rp["gup"].T,
              r_k_c=head_col(rwkv_r_k[0]), ln_w_c=head_col(rwkv_ln_w[0]), ln_b_c=head_col(rwkv_ln_b[0]))
    fp = dict(w_out=w_out[0].astype(BF16), g_ffn=row(g_ffn_norm[0]), w_up=w_ffn_up[0].astype(BF16),
              conv_w=ffn_conv_w[0], conv_b=row(ffn_conv_b[0]), w_down=w_ffn_down[0].astype(BF16),
              g_final=row(g_final_norm))

    rows = nb * t_len
    xp = x_prompt.reshape(rows, D_MODEL)
    mod_p = _mod(jnp.pad(c_prompt, ((0, 8 - nb), (0, 0))), w_ada_bf, b_ada2).reshape(8, 1, 6 * D_MODEL)
    zr, qb, kb, vb, kt_p, vt_p, lft = _inproj_prompt(xp, mod_p, nb, t_len, g_attn, w_in_bf, bfp)
    cum = _cumsum(lft)
    o_r, st = _rwkv_prompt(zr, jnp.zeros((nb, 1, ZR_PAD), F32), nb, t_len, rp)
    o_f = _attn_prompt(qb, kb, vb, cum, nb, t_len)
    y_p, conv_p = _ffn_prompt(xp, o_r, o_f, mod_p, jnp.zeros((nb, CONV_W - 1, 2 * D_FF), F32), fp, nb, t_len)

    heads_t = lambda a: jnp.transpose(a.reshape(nb, N_HEADS, HEAD_DIM, t_len), (0, 3, 1, 2))[None]
    y_prompt = y_p.reshape(nb, t_len, D_MODEL)
    k_prompt = heads_t(kt_p)
    v_prompt = heads_t(vt_p)
    logf_prompt = jnp.transpose(lft, (0, 2, 1))[None]
    wkv_prompt = jnp.swapaxes(st, -1, -2).reshape(1, nb, N_HEADS, HEAD_DIM, HEAD_DIM)
    shift_prompt = zr.reshape(nb, t_len, ZR_PAD)[:, -1, :D_RWKV_IN][None]
    conv_prompt = conv_p[None]

    xs = x_sample.reshape(db, D_MODEL)
    mod_s = _mod(c_sample, w_ada_bf, b_ada2)
    zrt_s, qb_s, k_s, v_s, kt_s, vt_s, lft_s = _inproj_sample(xs, mod_s, g_attn, w_in_bf, bfp)
    shift_t = jnp.pad(state_shift[0].T, ((0, ZR_PAD - D_RWKV_IN), (0, 0)))
    vecs = _rwkv_prep(zrt_s, shift_t, rp)
    hd3 = lambda a: a.reshape(N_HEADS, HEAD_DIM, db)
    o_r_t, wkv_t = _rwkv_step([hd3(a) for a in vecs], rp, jnp.transpose(state_wkv[0], (1, 2, 3, 0)))
    kc = jnp.transpose(cache_k[0], (0, 2, 3, 1))
    vc = jnp.transpose(cache_v[0], (0, 2, 3, 1))
    lfc = jnp.transpose(cache_logf[0], (0, 2, 1))
    to3 = lambda a: a.reshape(db, N_HEADS, HEAD_DIM)
    o_f_s = _attn_sample(to3(qb_s), to3(k_s), to3(v_s), lft_s.T.reshape(db, 1, N_HEADS), kc, vc, lfc,
                         page_table.T)
    cp = state_ffn_conv[0]
    y_s, u_s = _ffn_sample(xs, o_r_t.reshape(D_GROUP, db), o_f_s.reshape(db, D_GROUP), mod_s,
                           cp[:, 0, :], cp[:, 1, :], fp)

    heads_s = lambda a: jnp.transpose(hd3(a), (2, 0, 1)).reshape(1, db, 1, N_HEADS, HEAD_DIM)
    y_sample = y_s.reshape(db, 1, D_MODEL)
    k_sample = heads_s(kt_s)
    v_sample = heads_s(vt_s)
    logf_sample = lft_s.T.reshape(1, db, 1, N_HEADS)
    wkv_sample = jnp.transpose(wkv_t, (3, 0, 1, 2))[None]
    shift_sample = zrt_s[:D_RWKV_IN].T[None]
    conv_sample = jnp.stack([cp[:, 1, :], u_s], axis=1)[None]
    return (y_prompt, y_sample, k_prompt, v_prompt, logf_prompt, wkv_prompt, shift_prompt, conv_prompt,
            k_sample, v_sample, logf_sample, wkv_sample, shift_sample, conv_sample)
```

```python
import functools

import jax
import jax.numpy as jnp
from jax import lax
from jax.experimental import pallas as pl
from jax.experimental.pallas import tpu as pltpu

F32 = jnp.float32
BF16 = jnp.bfloat16

D_MODEL = 1024
HEAD_DIM = 64
N_HEADS = 8
D_GROUP = N_HEADS * HEAD_DIM
W_LORA, A_LORA, G_LORA = 32, 32, 96
D_RWKV_IN = 3 * D_GROUP + W_LORA + A_LORA + G_LORA
ZR_PAD = 1792
LORA_OFF = 3 * D_GROUP
LORA_W = ZR_PAD - LORA_OFF
F_PAD = 128
W_IN_PAD = ZR_PAD + 3 * D_GROUP + F_PAD
D_FF = 2816
CONV_W = 3
PAGE = 128
RMS_EPS = 1e-6
GN_EPS = 64e-5
ATTN_SCALE = HEAD_DIM ** -0.5
LOG2E = 1.4426950408889634
LANES = 128
VMEM_LIMIT = 56 * 1024 * 1024

ROW_TILE = 256
RWKV_TILE = 256
CHUNK = 64
ATTN_TILE = 1024
ATTN_KEYS = 2048
FF_CHUNK = D_FF
PAGES_PER_STEP = 16
PAGE_SLOTS = 3
FFN_TILE = 512


def _params(**kw):
    return pltpu.CompilerParams(vmem_limit_bytes=VMEM_LIMIT, **kw)


def _dot(a, b):
    return jnp.dot(a, b, preferred_element_type=F32)


def _dot_nt(a, b):
    return lax.dot_general(a, b, (((1,), (1,)), ((), ())), preferred_element_type=F32)


def _dot_tn(a, b):
    return lax.dot_general(a, b, (((0,), (0,)), ((), ())), preferred_element_type=F32)


def _bdot(a, b):
    return lax.dot_general(a, b, (((2,), (1,)), ((0,), (0,))), preferred_element_type=F32)


def _bdot_nt(a, b):
    return lax.dot_general(a, b, (((2,), (2,)), ((0,), (0,))), preferred_element_type=F32)


def _bdot_tn(a, b):
    return lax.dot_general(a, b, (((1,), (1,)), ((0,), (0,))), preferred_element_type=F32)


def _split3(x):
    hi = x.astype(BF16)
    r1 = x - hi.astype(F32)
    mid = r1.astype(BF16)
    lo = (r1 - mid.astype(F32)).astype(BF16)
    return hi, mid, lo


def _rms(x, g):
    return x * lax.rsqrt(jnp.mean(x * x, axis=-1, keepdims=True) + RMS_EPS) * g


def _mod_kernel(c_ref, w_ref, b_ref, o_ref):
    c = c_ref[...]
    s = (c * jax.nn.sigmoid(c)).astype(BF16)
    o_ref[...] = _dot(s, w_ref[...]) + b_ref[...]


def _mod(c, w_bf, b):
    m, n, tn = c.shape[0], w_bf.shape[1], 1536
    return pl.pallas_call(
        _mod_kernel, grid=(n // tn,),
        in_specs=[pl.BlockSpec((m, D_MODEL), lambda j: (0, 0)),
                  pl.BlockSpec((D_MODEL, tn), lambda j: (0, j)),
                  pl.BlockSpec((1, tn), lambda j: (0, j))],
        out_specs=pl.BlockSpec((m, tn), lambda j: (0, j)),
        out_shape=jax.ShapeDtypeStruct((m, n), F32),
        compiler_params=_params(), name="adaln_mod")(c, w_bf, b)


def _project(per_row, x_ref, sh_ref, sc_ref, g_ref, w_ref, bf_ref):
    sh = sh_ref[...] if per_row else sh_ref[0]
    sc = sc_ref[...] if per_row else sc_ref[0]
    h = (_rms(x_ref[...], g_ref[...]) * (1 + sc) + sh).astype(BF16)
    o = ZR_PAD
    zr = _dot(h, w_ref[:, 0:o])
    q = _dot(h, w_ref[:, o:o + D_GROUP]) * ATTN_SCALE
    k = _dot(h, w_ref[:, o + D_GROUP:o + 2 * D_GROUP])
    v = _dot(h, w_ref[:, o + 2 * D_GROUP:o + 3 * D_GROUP])
    f = _dot(h, w_ref[:, o + 3 * D_GROUP:o + 3 * D_GROUP + F_PAD])
    return zr, q, k, v, jax.nn.log_sigmoid(f + bf_ref[...])


def _inproj_prompt_kernel(x_ref, sh_ref, sc_ref, g_ref, w_ref, bf_ref,
                          zr_ref, qb_ref, kb_ref, vb_ref, kt_ref, vt_ref, lft_ref, k_s, v_s):
    zr, q, k, v, lf = _project(False, x_ref, sh_ref, sc_ref, g_ref, w_ref, bf_ref)
    zr_ref[...] = zr
    qb_ref[...] = (q * LOG2E).astype(BF16)
    kb_ref[...] = k.astype(BF16)
    vb_ref[...] = v.astype(BF16)
    k_s[...] = k
    v_s[...] = v
    kt_ref[0] = k_s[...].T
    vt_ref[0] = v_s[...].T
    lft_ref[0] = lf.T[0:N_HEADS, :]


def _inproj_sample_kernel(x_ref, sh_ref, sc_ref, g_ref, w_ref, bf_ref,
                          zrt_ref, qb_ref, k_ref, v_ref, kt_ref, vt_ref, lft_ref, zr_s):
    zr, q, k, v, lf = _project(True, x_ref, sh_ref, sc_ref, g_ref, w_ref, bf_ref)
    qb_ref[...] = q.astype(BF16)
    k_ref[...] = k
    v_ref[...] = v
    zr_s[...] = zr
    zrt_ref[...] = zr_s[...].T
    kt_ref[...] = k_ref[...].T
    vt_ref[...] = v_ref[...].T
    lft_ref[...] = lf.T[0:N_HEADS, :]


def _inproj_prompt(x2d, mod3, nb, t_len, g, w_bf, bfp):
    tm = ROW_TILE
    tpb = t_len // tm
    rows = nb * t_len
    mod_spec = lambda j: pl.BlockSpec((1, 1, D_MODEL), lambda i: (i // tpb, 0, j))
    row_spec = lambda n: pl.BlockSpec((tm, n), lambda i: (i, 0))
    col_spec = lambda n: pl.BlockSpec((1, n, tm), lambda i: (i // tpb, 0, i % tpb))
    const = lambda shape: pl.BlockSpec(shape, lambda i: (0, 0))
    return pl.pallas_call(
        _inproj_prompt_kernel, grid=(rows // tm,),
        in_specs=[row_spec(D_MODEL), mod_spec(0), mod_spec(1), const((1, D_MODEL)),
                  const((D_MODEL, W_IN_PAD)), const((1, F_PAD))],
        out_specs=[row_spec(ZR_PAD), row_spec(D_GROUP), row_spec(D_GROUP), row_spec(D_GROUP),
                   col_spec(D_GROUP), col_spec(D_GROUP), col_spec(N_HEADS)],
        out_shape=[jax.ShapeDtypeStruct((rows, ZR_PAD), F32),
                   jax.ShapeDtypeStruct((rows, D_GROUP), BF16),
                   jax.ShapeDtypeStruct((rows, D_GROUP), BF16),
                   jax.ShapeDtypeStruct((rows, D_GROUP), BF16),
                   jax.ShapeDtypeStruct((nb, D_GROUP, t_len), F32),
                   jax.ShapeDtypeStruct((nb, D_GROUP, t_len), F32),
                   jax.ShapeDtypeStruct((nb, N_HEADS, t_len), F32)],
        scratch_shapes=[pltpu.VMEM((tm, D_GROUP), F32), pltpu.VMEM((tm, D_GROUP), F32)],
        compiler_params=_params(dimension_semantics=("parallel",)), name="in_proj")(
            x2d, mod3, mod3, g, w_bf, bfp)


def _inproj_sample(x2d, mod, g, w_bf, bfp):
    m = x2d.shape[0]
    full = lambda r, c: pl.BlockSpec((r, c), lambda i: (0, 0))
    mod_spec = lambda j: pl.BlockSpec((m, D_MODEL), lambda i: (0, j))
    return pl.pallas_call(
        _inproj_sample_kernel, grid=(1,),
        in_specs=[full(m, D_MODEL), mod_spec(0), mod_spec(1), full(1, D_MODEL), full(D_MODEL, W_IN_PAD),
                  full(1, F_PAD)],
        out_specs=[full(ZR_PAD, m), full(m, D_GROUP), full(m, D_GROUP), full(m, D_GROUP),
                   full(D_GROUP, m), full(D_GROUP, m), full(N_HEADS, m)],
        out_shape=[jax.ShapeDtypeStruct((ZR_PAD, m), F32),
                   jax.ShapeDtypeStruct((m, D_GROUP), BF16),
                   jax.ShapeDtypeStruct((m, D_GROUP), F32),
                   jax.ShapeDtypeStruct((m, D_GROUP), F32),
                   jax.ShapeDtypeStruct((D_GROUP, m), F32),
                   jax.ShapeDtypeStruct((D_GROUP, m), F32),
                   jax.ShapeDtypeStruct((N_HEADS, m), F32)],
        scratch_shapes=[pltpu.VMEM((m, ZR_PAD), F32)],
        compiler_params=_params(), name="in_proj_sample")(x2d, mod, mod, g, w_bf, bfp)


def _cumsum_kernel(x_ref, o_ref):
    nb, _, t_len = x_ref.shape
    blk = 256
    r = lax.broadcasted_iota(jnp.int32, (blk, blk), 0)
    c = lax.broadcasted_iota(jnp.int32, (blk, blk), 1)
    upper = (r <= c).astype(BF16)
    for b in range(nb):
        carry = jnp.zeros((N_HEADS, 1), F32)
        for j in range(t_len // blk):
            sl = slice(j * blk, (j + 1) * blk)
            hi, mid, lo = _split3(x_ref[b, :, sl])
            cs = (_dot(lo, upper) + _dot(mid, upper)) + _dot(hi, upper) + carry
            o_ref[b, :, sl] = cs
            carry = cs[:, blk - 1:blk]


def _cumsum(lft):
    return pl.pallas_call(
        _cumsum_kernel, out_shape=jax.ShapeDtypeStruct(lft.shape, F32),
        compiler_params=_params(), name="logf_cumsum")(lft)


def _rwkv_mix(zm, w0, wup, a0, aup, gup, k_k, k_a):
    r = zm[:, 0:D_GROUP]
    k = zm[:, D_GROUP:2 * D_GROUP]
    v = zm[:, 2 * D_GROUP:3 * D_GROUP]
    lora = zm[:, LORA_OFF:ZR_PAD]
    w = w0 + _dot(jnp.tanh(lora).astype(BF16), wup)
    logd = -jnp.exp(-jax.nn.softplus(-w) - 0.5)
    a = jax.nn.sigmoid(a0 + _dot(lora.astype(BF16), aup))
    g = _dot(jax.nn.sigmoid(lora).astype(BF16), gup)
    kk = k * k_k
    k = k * (1 + (a - 1) * k_a)
    return r, k, v, logd, kk, a, g


def _group_norm(y):
    mean = jnp.mean(y, axis=-1, keepdims=True)
    var = jnp.mean(jnp.square(y - mean), axis=-1, keepdims=True)
    return (y - mean) * lax.rsqrt(var + GN_EPS)


def _rwkv_kernel(z_ref, sh0_ref, mu_ref, w0_ref, wup_ref, a0_ref, aup_ref, gup_ref, kk_ref, ka_ref,
                 rk_ref, lnw_ref, lnb_ref, o_ref, st_ref, prev_s, st_s, hm_s, yn_s, bon_s):
    tt, c = RWKV_TILE, CHUNK
    t = pl.program_id(1)

    @pl.when(t == 0)
    def _():
        prev_s[0:1, :] = sh0_ref[0]
        st_s[...] = jnp.zeros_like(st_s)

    z = z_ref[...]
    row = lax.broadcasted_iota(jnp.int32, (tt, 1), 0)
    zprev = jnp.where(row == 0, prev_s[0:1, :], pltpu.roll(z, 1, axis=0))
    prev_s[0:1, :] = z[tt - 1:tt, :]
    zm = z + mu_ref[...] * (zprev - z)
    r, k, v, logd, kk, a, g = _rwkv_mix(zm, w0_ref[...], wup_ref[...], a0_ref[...], aup_ref[...],
                                        gup_ref[...], kk_ref[...], ka_ref[...])
    ri = lax.broadcasted_iota(jnp.int32, (tt, tt), 0)
    ci = lax.broadcasted_iota(jnp.int32, (tt, tt), 1)
    tri = ((ri // c == ci // c) & (ci <= ri)).astype(BF16)
    hi, mid, lo = _split3(logd)
    cl = (_dot(tri, lo) + _dot(tri, mid)) + _dot(tri, hi)
    for h in range(N_HEADS):
        sl = slice(h * HEAD_DIM, (h + 1) * HEAD_DIM)
        for n, val in enumerate((r, k, v, kk, a, logd, cl)):
            hm_s[n, h] = val[:, sl]

    rr = lax.broadcasted_iota(jnp.int32, (c, c), 0)
    cc = lax.broadcasted_iota(jnp.int32, (c, c), 1)
    strict = (cc < rr).astype(F32)
    eye = rr == cc
    r2 = lax.broadcasted_iota(jnp.int32, (c, 2 * c), 0)
    c2 = lax.broadcasted_iota(jnp.int32, (c, 2 * c), 1)
    sgn = jnp.where(c2 < c, jnp.where(c2 <= r2, 1.0, 0.0), jnp.where(c2 - c <= r2, -1.0, 0.0)).astype(F32)
    nch = tt // c
    grp = N_HEADS * nch
    r_, k_, v_, kk_, a_, ld_, cl_ = (hm_s[n].reshape(grp, c, HEAD_DIM) for n in range(7))
    nrm = jnp.sqrt(jnp.sum(kk_ * kk_, axis=-1, keepdims=True))
    kkn = kk_ / jnp.maximum(nrm, 1e-12)
    b_ = kkn * a_
    cl_end = cl_[:, c - 1:c, :]
    e_end = jnp.exp(cl_end - cl_)
    g_inv = jnp.exp(-cl_)
    kkg = kkn * jnp.exp(cl_ - ld_)
    rg = r_ * jnp.exp(cl_)
    kd = (k_ * g_inv).astype(BF16)
    bd = (b_ * g_inv).astype(BF16)
    kkg_b = kkg.astype(BF16)
    a_k = _bdot_nt(kkg_b, kd) * strict
    a_b = _bdot_nt(kkg_b, bd) * strict
    ll = _bdot_nt(rg.astype(BF16), jnp.concatenate([kd, bd], axis=1)) * sgn
    akv = _bdot(a_k.astype(BF16), v_.astype(BF16))
    x = jnp.concatenate([akv, kkg], axis=2)
    p = (-a_b).astype(BF16)
    x = x + _bdot(p, x.astype(BF16))
    n = 2
    while n < c:
        p = _bdot(p, p).astype(BF16)
        x = x + _bdot(p, x.astype(BF16))
        n *= 2
    rhs = jnp.concatenate([jnp.concatenate([v_, jnp.zeros_like(v_)], axis=2), x], axis=1).astype(BF16)
    yq = _bdot(ll.astype(BF16), rhs)
    kb = jnp.concatenate([k_ * e_end, -(b_ * e_end)], axis=1).astype(BF16)
    sw = _bdot_tn(kb, rhs)
    qc = (rg + yq[:, :, HEAD_DIM:]).astype(BF16)
    m = (jnp.where(eye, jnp.broadcast_to(jnp.exp(cl_end), (grp, c, c)), 0.0) + sw[:, :, HEAD_DIM:]).astype(BF16)
    by_head = lambda t: t.reshape(N_HEADS, nch, *t.shape[1:])
    qc, m, y0, w = by_head(qc), by_head(m), by_head(yq[:, :, 0:HEAD_DIM]), by_head(sw[:, :, 0:HEAD_DIM])
    st = st_s[...]
    ys = []
    for i in range(nch):
        st_b = st.astype(BF16)
        ys.append(_bdot(qc[:, i], st_b) + y0[:, i])
        st = _bdot(m[:, i], st_b) + w[:, i]
    st_s[...] = st
    yn_s[...] = _group_norm(jnp.concatenate(ys, axis=1))
    bon_s[...] = jnp.sum(hm_s[0] * hm_s[1] * rk_ref[...], axis=-1, keepdims=True) * hm_s[2]
    yn = jnp.concatenate([yn_s[h] for h in range(N_HEADS)], axis=1)
    bon = jnp.concatenate([bon_s[h] for h in range(N_HEADS)], axis=1)
    o_ref[...] = ((yn * lnw_ref[...] + lnb_ref[...] + bon) * g).astype(BF16)
    st_ref[...] = st_s[...]


def _rwkv_prompt(zr, shift0, nb, t_len, rp):
    tt = RWKV_TILE
    nt = t_len // tt
    const = lambda shape: pl.BlockSpec(shape, lambda b, t: (0,) * len(shape))
    return pl.pallas_call(
        _rwkv_kernel, grid=(nb, nt),
        in_specs=[pl.BlockSpec((tt, ZR_PAD), lambda b, t: (b * nt + t, 0)),
                  pl.BlockSpec((1, 1, ZR_PAD), lambda b, t: (b, 0, 0)),
                  const((1, ZR_PAD)), const((1, D_GROUP)), const((LORA_W, D_GROUP)),
                  const((1, D_GROUP)), const((LORA_W, D_GROUP)), const((LORA_W, D_GROUP)),
                  const((1, D_GROUP)), const((1, D_GROUP)), const((N_HEADS, 1, HEAD_DIM)),
                  const((1, D_GROUP)), const((1, D_GROUP))],
        out_specs=[pl.BlockSpec((tt, D_GROUP), lambda b, t: (b * nt + t, 0)),
                   pl.BlockSpec((N_HEADS, HEAD_DIM, HEAD_DIM), lambda b, t: (b, 0, 0))],
        out_shape=[jax.ShapeDtypeStruct((nb * t_len, D_GROUP), BF16),
                   jax.ShapeDtypeStruct((nb * N_HEADS, HEAD_DIM, HEAD_DIM), F32)],
        scratch_shapes=[pltpu.VMEM((8, ZR_PAD), F32),
                        pltpu.VMEM((N_HEADS, HEAD_DIM, HEAD_DIM), F32),
                        pltpu.VMEM((7, N_HEADS, tt, HEAD_DIM), F32),
                        pltpu.VMEM((N_HEADS, tt, HEAD_DIM), F32),
                        pltpu.VMEM((N_HEADS, tt, HEAD_DIM), F32)],
        compiler_params=_params(dimension_semantics=("arbitrary", "arbitrary")), name="rwkv_scan")(
            zr, shift0, rp["mu"], rp["w0"], rp["wup"], rp["a0"], rp["aup"], rp["gup"], rp["k_k"],
            rp["k_a"], rp["r_k3"], rp["ln_w"], rp["ln_b"])


def _attn_kernel(q_ref, k_ref, v_ref, c_ref, o_ref):
    tq, tk = ATTN_TILE, ATTN_KEYS
    qi = pl.program_id(2)
    q_start = qi * tq
    q2 = q_ref[...]
    lane = lax.broadcasted_iota(jnp.int32, (1, LANES), 1)
    lo = lane < HEAD_DIM
    zero = jnp.zeros_like(q2)
    q_h = (jnp.where(lo, q2, zero), jnp.where(lo, zero, q2))
    c_ref0 = c_ref[0, 0, :, pl.ds(pl.multiple_of(q_start, tq), LANES)][:, 0:1]

    def step(off, width, carry, masked):
        kb = k_ref[pl.ds(off, width), :]
        vb = v_ref[pl.ds(off, width), :]
        ck = c_ref[0, 0, :, pl.ds(off, width)]
        if masked:
            row = lax.broadcasted_iota(jnp.int32, (tq, width), 0)
            col = lax.broadcasted_iota(jnp.int32, (tq, width), 1)
            causal = col + off <= row + q_start
        out = []
        for hd in range(2):
            m, l, acc = carry[3 * hd:3 * hd + 3]
            s = _dot_nt(q_h[hd], kb) + (c_ref0[hd:hd + 1, :] - ck[hd:hd + 1, :]) * LOG2E
            if masked:
                s = jnp.where(causal, s, -jnp.inf)
            m_new = jnp.maximum(m, jnp.max(s, axis=-1, keepdims=True))
            alpha = jnp.exp2(m - m_new)
            p = jnp.exp2(s - m_new)
            l = alpha * l + jnp.sum(p, axis=-1, keepdims=True)
            acc = alpha * acc + _dot(p.astype(BF16), vb)
            out += [m_new, l, acc]
        return tuple(out)

    init = (jnp.full((tq, 1), -jnp.inf, F32), jnp.zeros((tq, 1), F32), jnp.zeros((tq, LANES), F32)) * 2
    n_full = q_start // tk
    carry = lax.fori_loop(0, n_full, lambda j, cr: step(pl.multiple_of(j * tk, tk), tk, cr, False), init)
    tail = pl.multiple_of(n_full * tk, tk)
    pieces = (q_start + tq - tail) // tq
    carry = lax.switch(pieces - 1, [functools.partial(step, tail, (n + 1) * tq, masked=True)
                                    for n in range(tk // tq)], carry)
    _, l0, acc0, _, l1, acc1 = carry
    o_ref[...] = jnp.where(lo, acc0 / l0, acc1 / l1).astype(BF16)


def _attn_prompt(qb, kb, vb, cum, nb, t_len):
    tq = ATTN_TILE
    nq = t_len // tq
    n_pairs = D_GROUP // LANES
    cum4 = cum.reshape(nb, n_pairs, 2, t_len)
    return pl.pallas_call(
        _attn_kernel, grid=(nb, n_pairs, nq),
        in_specs=[pl.BlockSpec((tq, LANES), lambda b, hp, qi: (b * nq + qi, hp)),
                  pl.BlockSpec((t_len, LANES), lambda b, hp, qi: (b, hp)),
                  pl.BlockSpec((t_len, LANES), lambda b, hp, qi: (b, hp)),
                  pl.BlockSpec((1, 1, 2, t_len), lambda b, hp, qi: (b, hp, 0, 0))],
        out_specs=pl.BlockSpec((tq, LANES), lambda b, hp, qi: (b * nq + qi, hp)),
        out_shape=jax.ShapeDtypeStruct((nb * t_len, D_GROUP), BF16),
        compiler_params=_params(dimension_semantics=("parallel", "parallel", "arbitrary")),
        name="fox_prompt")(qb, kb, vb, cum4)


def _ffn_front(x, o_r, o_f, gta, shf, scf, wout_ref, gffn):
    attn = _dot(o_r, wout_ref[0:D_GROUP, :]) + _dot(o_f, wout_ref[D_GROUP:2 * D_GROUP, :])
    x1 = x + gta * attn
    return x1, (_rms(x1, gffn) * (1 + scf) + shf).astype(BF16)


def _ffn_prompt_kernel(tpb, x_ref, or_ref, of_ref, gta_ref, shf_ref, scf_ref, gtf_ref, wout_ref, gffn_ref,
                       wup_ref, cw_ref, cb_ref, wdn_ref, gfin_ref, cp_ref, y_ref, conv_ref, prev_s):
    i = pl.program_id(0)

    @pl.when(i % tpb == 0)
    def _():
        prev_s[0:2, :] = cp_ref[0]

    x1, h2 = _ffn_front(x_ref[...], or_ref[...], of_ref[...], gta_ref[0], shf_ref[0], scf_ref[0],
                        wout_ref, gffn_ref[...])
    tm = x1.shape[0]
    row = lax.broadcasted_iota(jnp.int32, (tm, 1), 0)

    def conv(cols):
        u = _dot(h2, wup_ref[:, cols])
        p0 = prev_s[0:1, cols]
        p1 = prev_s[1:2, cols]
        u1 = jnp.where(row == 0, p1, pltpu.roll(u, 1, axis=0))
        u2 = jnp.where(row == 0, p0, jnp.where(row == 1, p1, pltpu.roll(u, 2, axis=0)))
        prev_s[0:2, cols] = u[tm - 2:tm, :]
        return cb_ref[:, cols] + cw_ref[0:1, cols] * u2 + cw_ref[1:2, cols] * u1 + cw_ref[2:3, cols] * u

    acc = jnp.zeros((tm, D_MODEL), F32)
    for j in range(D_FF // FF_CHUNK):
        gate = conv(slice(j * FF_CHUNK, (j + 1) * FF_CHUNK))
        val = conv(slice(D_FF + j * FF_CHUNK, D_FF + (j + 1) * FF_CHUNK))
        act = (gate * jax.nn.sigmoid(gate) * val).astype(BF16)
        acc = acc + _dot(act, wdn_ref[j * FF_CHUNK:(j + 1) * FF_CHUNK, :])
    y_ref[...] = _rms(x1 + gtf_ref[0] * acc, gfin_ref[...])
    conv_ref[0] = prev_s[0:2, :]


def _ffn_prompt(x2d, o_r, o_f, mod3, conv0, fp, nb, t_len):
    tm = FFN_TILE
    tpb = t_len // tm
    rows = nb * t_len
    row_spec = lambda n: pl.BlockSpec((tm, n), lambda i: (i, 0))
    mod_spec = lambda j: pl.BlockSpec((1, 1, D_MODEL), lambda i: (i // tpb, 0, j))
    const = lambda shape: pl.BlockSpec(shape, lambda i: (0, 0))
    weight = lambda shape: pl.BlockSpec(shape, lambda i: (0, 0), pipeline_mode=pl.Buffered(1))
    conv_spec = pl.BlockSpec((1, CONV_W - 1, 2 * D_FF), lambda i: (i // tpb, 0, 0))
    return pl.pallas_call(
        functools.partial(_ffn_prompt_kernel, tpb), grid=(rows // tm,),
        in_specs=[row_spec(D_MODEL), row_spec(D_GROUP), row_spec(D_GROUP),
                  mod_spec(2), mod_spec(3), mod_spec(4), mod_spec(5),
                  weight((D_MODEL, D_MODEL)), const((1, D_MODEL)), weight((D_MODEL, 2 * D_FF)),
                  const((CONV_W, 2 * D_FF)), const((1, 2 * D_FF)), weight((D_FF, D_MODEL)),
                  const((1, D_MODEL)), conv_spec],
        out_specs=[row_spec(D_MODEL), conv_spec],
        out_shape=[jax.ShapeDtypeStruct((rows, D_MODEL), F32),
                   jax.ShapeDtypeStruct((nb, CONV_W - 1, 2 * D_FF), F32)],
        scratch_shapes=[pltpu.VMEM((8, 2 * D_FF), F32)],
        compiler_params=_params(dimension_semantics=("arbitrary",)), name="ffn_prompt")(
            x2d, o_r, o_f, mod3, mod3, mod3, mod3, fp["w_out"], fp["g_ffn"], fp["w_up"], fp["conv_w"],
            fp["conv_b"], fp["w_down"], fp["g_final"], conv0)


def _ffn_sample_kernel(x_ref, ort_ref, of_ref, gta_ref, shf_ref, scf_ref, gtf_ref, wout_ref, gffn_ref,
                       wup_ref, cw_ref, cb_ref, wdn_ref, gfin_ref, cp0_ref, cp1_ref, y_ref, u_ref):
    attn = (_dot(ort_ref[...].T.astype(BF16), wout_ref[0:D_GROUP, :])
            + _dot(of_ref[...].astype(BF16), wout_ref[D_GROUP:2 * D_GROUP, :]))
    x1 = x_ref[...] + gta_ref[...] * attn
    h2 = (_rms(x1, gffn_ref[...]) * (1 + scf_ref[...]) + shf_ref[...]).astype(BF16)

    def conv(cols):
        u = _dot(h2, wup_ref[:, cols])
        u_ref[:, cols] = u
        return (cb_ref[:, cols] + cw_ref[0:1, cols] * cp0_ref[:, cols] + cw_ref[1:2, cols] * cp1_ref[:, cols]
                + cw_ref[2:3, cols] * u)

    acc = jnp.zeros(x1.shape, F32)
    for j in range(D_FF // FF_CHUNK):
        gate = conv(slice(j * FF_CHUNK, (j + 1) * FF_CHUNK))
        val = conv(slice(D_FF + j * FF_CHUNK, D_FF + (j + 1) * FF_CHUNK))
        act = (gate * jax.nn.sigmoid(gate) * val).astype(BF16)
        acc = acc + _dot(act, wdn_ref[j * FF_CHUNK:(j + 1) * FF_CHUNK, :])
    y_ref[...] = _rms(x1 + gtf_ref[...] * acc, gfin_ref[...])


def _ffn_sample(x2d, o_r_t, o_f, mod, cp0, cp1, fp):
    m = x2d.shape[0]
    full = lambda n: pl.BlockSpec((m, n), lambda i: (0, 0))
    mod_spec = lambda j: pl.BlockSpec((m, D_MODEL), lambda i: (0, j))
    const = lambda shape: pl.BlockSpec(shape, lambda i: (0, 0))
    return pl.pallas_call(
        _ffn_sample_kernel, grid=(1,),
        in_specs=[full(D_MODEL), const((D_GROUP, m)), full(D_GROUP),
                  mod_spec(2), mod_spec(3), mod_spec(4), mod_spec(5),
                  const((D_MODEL, D_MODEL)), const((1, D_MODEL)), const((D_MODEL, 2 * D_FF)),
                  const((CONV_W, 2 * D_FF)), const((1, 2 * D_FF)), const((D_FF, D_MODEL)),
                  const((1, D_MODEL)), full(2 * D_FF), full(2 * D_FF)],
        out_specs=[full(D_MODEL), full(2 * D_FF)],
        out_shape=[jax.ShapeDtypeStruct((m, D_MODEL), F32), jax.ShapeDtypeStruct((m, 2 * D_FF), F32)],
        compiler_params=_params(), name="ffn_sample")(
            x2d, o_r_t, o_f, mod, mod, mod, mod, fp["w_out"], fp["g_ffn"], fp["w_up"], fp["conv_w"],
            fp["conv_b"], fp["w_down"], fp["g_final"], cp0, cp1)


def _rwkv_prep_kernel(zt_ref, sht_ref, mu_ref, w0_ref, wupt_ref, a0_ref, aupt_ref, gupt_ref, kk_ref, ka_ref,
                      r_ref, k_ref, v_ref, d_ref, kko_ref, a_ref, g_ref):
    z = zt_ref[...]
    zm = z + mu_ref[...] * (sht_ref[...] - z)
    k = zm[D_GROUP:2 * D_GROUP, :]
    lora = zm[LORA_OFF:ZR_PAD, :]
    w = w0_ref[...] + _dot(wupt_ref[...], jnp.tanh(lora).astype(BF16))
    a = jax.nn.sigmoid(a0_ref[...] + _dot(aupt_ref[...], lora.astype(BF16)))
    r_ref[...] = zm[0:D_GROUP, :]
    k_ref[...] = k * (1 + (a - 1) * ka_ref[...])
    v_ref[...] = zm[2 * D_GROUP:3 * D_GROUP, :]
    d_ref[...] = jnp.exp(-jnp.exp(-jax.nn.softplus(-w) - 0.5))
    kko_ref[...] = k * kk_ref[...]
    a_ref[...] = a
    g_ref[...] = _dot(gupt_ref[...], jax.nn.sigmoid(lora).astype(BF16))


def _rwkv_prep(zrt, shift_t, rp):
    m = zrt.shape[1]
    return pl.pallas_call(
        _rwkv_prep_kernel,
        out_shape=[jax.ShapeDtypeStruct((D_GROUP, m), F32)] * 7,
        compiler_params=_params(), name="rwkv_prep")(
            zrt, shift_t, rp["mu_c"], rp["w0_c"], rp["wup_t"], rp["a0_c"], rp["aup_t"], rp["gup_t"],
            rp["k_k_c"], rp["k_a_c"])


def _rwkv_step_kernel(r_ref, k_ref, v_ref, d_ref, kk_ref, a_ref, g_ref, rk_ref, lnw_ref, lnb_ref, s_ref,
                      o_ref, so_ref, y_s):
    r, k, d, kk, a = (ref[0] for ref in (r_ref, k_ref, d_ref, kk_ref, a_ref))
    nrm = jnp.sqrt(jnp.sum(kk * kk, axis=0, keepdims=True))
    kkn = kk / jnp.maximum(nrm, 1e-12)
    bb = kkn * a

    def body(i, carry):
        s = s_ref[0, i]
        sa = -jnp.sum(s * kkn, axis=0, keepdims=True)
        s = s * d + sa * bb + v_ref[0, pl.ds(i, 1), :] * k
        so_ref[0, i] = s
        y_s[pl.ds(i, 1), :] = jnp.sum(s * r, axis=0, keepdims=True)
        return carry

    lax.fori_loop(0, HEAD_DIM, body, 0)
    y = y_s[...]
    mean = jnp.mean(y, axis=0, keepdims=True)
    var = jnp.mean(jnp.square(y - mean), axis=0, keepdims=True)
    yn = (y - mean) * lax.rsqrt(var + GN_EPS)
    bonus = jnp.sum(r * k * rk_ref[0], axis=0, keepdims=True) * v_ref[0]
    o_ref[0] = (yn * lnw_ref[0] + lnb_ref[0] + bonus) * g_ref[0]


def _rwkv_step(vecs, rp, state_t):
    m = state_t.shape[-1]
    vec_spec = pl.BlockSpec((1, HEAD_DIM, m), lambda h: (h, 0, 0))
    par_spec = pl.BlockSpec((1, HEAD_DIM, 1), lambda h: (h, 0, 0))
    st_spec = pl.BlockSpec((1, HEAD_DIM, HEAD_DIM, m), lambda h: (h, 0, 0, 0))
    return pl.pallas_call(
        _rwkv_step_kernel, grid=(N_HEADS,),
        in_specs=[vec_spec] * 7 + [par_spec] * 3 + [st_spec],
        out_specs=[vec_spec, st_spec],
        out_shape=[jax.ShapeDtypeStruct((N_HEADS, HEAD_DIM, m), F32),
                   jax.ShapeDtypeStruct(state_t.shape, F32)],
        scratch_shapes=[pltpu.VMEM((HEAD_DIM, m), F32)],
        compiler_params=_params(dimension_semantics=("parallel",)), name="rwkv_step")(
            *vecs, rp["r_k_c"], rp["ln_w_c"], rp["ln_b_c"], state_t)


def _lane_prefix(x):
    lane = lax.broadcasted_iota(jnp.int32, x.shape, 1)
    s = 1
    while s < x.shape[1]:
        x = x + jnp.where(lane >= s, pltpu.roll(x, s, axis=1), 0.0)
        s *= 2
    return x


def _pattn_kernel(npg, pt_ref, q_ref, kn_ref, vn_ref, lfn_ref, *refs):
    pg = PAGES_PER_STEP
    kc_ref, vc_ref, lfc_ref, o_ref, kbuf, vbuf, lfbuf, sem, m_s, l_s, c_s, qc_s, acc_s = refs
    g = pl.program_id(1)
    n_seq = pl.num_programs(0)
    step = pl.program_id(0) * npg + g
    last = n_seq * npg - 1
    slot = step % PAGE_SLOTS
    ahead = PAGE_SLOTS - 1

    def page_copies(st, sl, fetch):
        seq, grp = st // npg, st % npg
        out = []
        for p in range(pg):
            page = pt_ref[grp * pg + p, seq] if fetch else 0
            out.append(pltpu.make_async_copy(kc_ref.at[page], kbuf.at[sl, p], sem.at[sl, 0]))
            out.append(pltpu.make_async_copy(vc_ref.at[page], vbuf.at[sl, p], sem.at[sl, 1]))
            out.append(pltpu.make_async_copy(lfc_ref.at[page], lfbuf.at[sl, p], sem.at[sl, 2]))
        return out

    @pl.when(step == 0)
    def _():
        for t in range(ahead):
            for cp in page_copies(jnp.minimum(t, last), t, True):
                cp.start()

    for cp in page_copies(step, slot, False):
        cp.wait()
    for cp in page_copies(jnp.minimum(step + ahead, last), (step + ahead) % PAGE_SLOTS, True):
        cp.start()

    n = HEAD_DIM
    eye = lax.broadcasted_iota(jnp.int32, (n, n), 0) == lax.broadcasted_iota(jnp.int32, (n, n), 1)

    @pl.when(g == 0)
    def _():
        m_s[...] = jnp.full_like(m_s, -jnp.inf)
        l_s[...] = jnp.zeros_like(l_s)
        c_s[...] = jnp.zeros_like(c_s)
        acc_s[...] = jnp.zeros_like(acc_s)
        q = q_ref[0].astype(F32)
        for h in range(N_HEADS):
            col = jnp.sum(jnp.where(eye, q[h:h + 1, :], 0.0), axis=-1, keepdims=True)
            qc_s[h] = jnp.broadcast_to(col, (n, PAGE))

    within = _lane_prefix(lfbuf[slot].reshape(pg * N_HEADS, PAGE))
    run = c_s[:, 0:1]
    scores = []
    for p in range(pg):
        w_p = within[p * N_HEADS:(p + 1) * N_HEADS, :]
        rows = [jnp.sum(qc_s[h] * kbuf[slot, p, h], axis=0, keepdims=True) for h in range(N_HEADS)]
        scores.append(jnp.concatenate(rows, axis=0) - (run + w_p))
        run = run + w_p[:, PAGE - 1:PAGE]
    c_s[...] = jnp.broadcast_to(run, c_s.shape)
    m_old = m_s[:, 0:1]
    m_new = m_old
    for s in scores:
        m_new = jnp.maximum(m_new, jnp.max(s, axis=-1, keepdims=True))
    alpha = jnp.exp(m_old - m_new)
    probs = [jnp.exp(s - m_new) for s in scores]
    l = alpha * l_s[:, 0:1]
    for pr in probs:
        l = l + jnp.sum(pr, axis=-1, keepdims=True)
    m_s[...] = jnp.broadcast_to(m_new, m_s.shape)
    l_s[...] = jnp.broadcast_to(l, l_s.shape)
    for h in range(N_HEADS):
        a = acc_s[h] * alpha[h:h + 1, :]
        for p in range(pg):
            a = a + probs[p][h:h + 1, :] * vbuf[slot, p, h]
        acc_s[h] = a

    @pl.when(step == last)
    def _():
        for t in range(1, PAGE_SLOTS):
            for cp in page_copies(step, (step + t) % PAGE_SLOTS, False):
                cp.wait()

    @pl.when(g == npg - 1)
    def _():
        rows = []
        for h in range(N_HEADS):
            col = jnp.sum(acc_s[h], axis=-1, keepdims=True)
            rows.append(jnp.sum(jnp.where(eye, col, 0.0), axis=0, keepdims=True))
        o_past = jnp.concatenate(rows, axis=0)
        r8 = lax.broadcasted_iota(jnp.int32, (N_HEADS, N_HEADS), 0)
        c8 = lax.broadcasted_iota(jnp.int32, (N_HEADS, N_HEADS), 1)
        lfn_col = jnp.sum(jnp.where(r8 == c8, lfn_ref[0], 0.0), axis=-1, keepdims=True)
        s_new = jnp.sum(q_ref[0].astype(F32) * kn_ref[0], axis=-1, keepdims=True) - (run + lfn_col)
        m_fin = jnp.maximum(m_new, s_new)
        beta = jnp.exp(m_new - m_fin)
        p_new = jnp.exp(s_new - m_fin)
        o_ref[0] = (o_past * beta + p_new * vn_ref[0]) / (l * beta + p_new)


def _attn_sample(qb3, kn3, vn3, lfn3, kc, vc, lfc, pt_t):
    n_pages, nb = pt_t.shape
    pg = PAGES_PER_STEP
    npg = n_pages // pg
    vec_spec = pl.BlockSpec((1, N_HEADS, HEAD_DIM), lambda b, g, pt: (b, 0, 0))
    hbm = pl.BlockSpec(memory_space=pl.ANY)
    grid_spec = pltpu.PrefetchScalarGridSpec(
        num_scalar_prefetch=1, grid=(nb, npg),
        in_specs=[vec_spec, vec_spec, vec_spec, pl.BlockSpec((1, 1, N_HEADS), lambda b, g, pt: (b, 0, 0)),
                  hbm, hbm, hbm],
        out_specs=vec_spec,
        scratch_shapes=[pltpu.VMEM((PAGE_SLOTS, pg, N_HEADS, HEAD_DIM, PAGE), F32),
                        pltpu.VMEM((PAGE_SLOTS, pg, N_HEADS, HEAD_DIM, PAGE), F32),
                        pltpu.VMEM((PAGE_SLOTS, pg, N_HEADS, PAGE), F32),
                        pltpu.SemaphoreType.DMA((PAGE_SLOTS, 3)),
                        pltpu.VMEM((N_HEADS, LANES), F32), pltpu.VMEM((N_HEADS, LANES), F32),
                        pltpu.VMEM((N_HEADS, LANES), F32),
                        pltpu.VMEM((N_HEADS, HEAD_DIM, PAGE), F32), pltpu.VMEM((N_HEADS, HEAD_DIM, PAGE), F32)])
    return pl.pallas_call(
        functools.partial(_pattn_kernel, npg), grid_spec=grid_spec,
        out_shape=jax.ShapeDtypeStruct((nb, N_HEADS, HEAD_DIM), F32),
        compiler_params=_params(dimension_semantics=("arbitrary", "arbitrary")), name="fox_sample")(
            pt_t, qb3, kn3, vn3, lfn3, kc, vc, lfc)


def _pad_cols(a, n):
    return jnp.pad(a, ((0, 0), (0, n - a.shape[1])))


def kernel(x_prompt, x_sample, c_prompt, c_sample, cache_k, cache_v, cache_logf, page_table, state_wkv,
           state_shift, state_ffn_conv, w_ada, b_ada, g_attn_norm, w_in, b_forget, rwkv_mu, rwkv_w0,
           rwkv_w_up, rwkv_a0, rwkv_a_up, rwkv_g_up, rwkv_k_k, rwkv_k_a, rwkv_r_k, rwkv_ln_w, rwkv_ln_b,
           w_out, g_ffn_norm, w_ffn_up, ffn_conv_w, ffn_conv_b, w_ffn_down, g_final_norm):
    depth = w_in.shape[0]
    assert depth == 1, "one layer per call"
    nb, t_len, _ = x_prompt.shape
    db, dt, _ = x_sample.shape
    assert dt == 1, "the sample group decodes one token per sequence"
    assert t_len % FFN_TILE == 0 and t_len % ATTN_TILE == 0 and ATTN_KEYS % ATTN_TILE == 0 and db % 8 == 0

    w_ada_bf = w_ada[0].astype(BF16)
    b_ada2 = b_ada[0][None, :]
    wi = w_in[0]
    o = D_RWKV_IN
    w_in_bf = jnp.concatenate(
        [_pad_cols(wi[:, :o], ZR_PAD), wi[:, o:o + 3 * D_GROUP], _pad_cols(wi[:, o + 3 * D_GROUP:], F_PAD)],
        axis=1).astype(BF16)
    bfp = _pad_cols(b_forget[0][None, :], F_PAD)
    g_attn = g_attn_norm[0][None, :]
    lora_rows = lambda w, off: jnp.pad(w, ((off, LORA_W - off - w.shape[0]), (0, 0))).astype(BF16)
    row = lambda a: a[None, :]
    col = lambda a: a[:, None]
    head_col = lambda a: a.reshape(N_HEADS, HEAD_DIM, 1)
    rp = dict(mu=_pad_cols(row(rwkv_mu[0]), ZR_PAD), w0=row(rwkv_w0[0]), a0=row(rwkv_a0[0]),
              wup=lora_rows(rwkv_w_up[0], 0), aup=lora_rows(rwkv_a_up[0], W_LORA),
              gup=lora_rows(rwkv_g_up[0], W_LORA + A_LORA),
              k_k=row(rwkv_k_k[0]), k_a=row(rwkv_k_a[0]), ln_w=row(rwkv_ln_w[0]), ln_b=row(rwkv_ln_b[0]),
              r_k3=rwkv_r_k[0][:, None, :])
    rp.update(mu_c=rp["mu"].T, w0_c=col(rwkv_w0[0]), a0_c=col(rwkv_a0[0]), k_k_c=col(rwkv_k_k[0]),
              k_a_c=col(rwkv_k_a[0]), wup_t=rp["wup"].T, aup_t=rp["aup"].T, gup_t=rp["gup"].T,
              r_k_c=head_col(rwkv_r_k[0]), ln_w_c=head_col(rwkv_ln_w[0]), ln_b_c=head_col(rwkv_ln_b[0]))
    fp = dict(w_out=w_out[0].astype(BF16), g_ffn=row(g_ffn_norm[0]), w_up=w_ffn_up[0].astype(BF16),
              conv_w=ffn_conv_w[0], conv_b=row(ffn_conv_b[0]), w_down=w_ffn_down[0].astype(BF16),
              g_final=row(g_final_norm))

    rows = nb * t_len
    xp = x_prompt.reshape(rows, D_MODEL)
    mod_p = _mod(jnp.pad(c_prompt, ((0, 8 - nb), (0, 0))), w_ada_bf, b_ada2).reshape(8, 1, 6 * D_MODEL)
    zr, qb, kb, vb, kt_p, vt_p, lft = _inproj_prompt(xp, mod_p, nb, t_len, g_attn, w_in_bf, bfp)
    cum = _cumsum(lft)
    o_r, st = _rwkv_prompt(zr, jnp.zeros((nb, 1, ZR_PAD), F32), nb, t_len, rp)
    o_f = _attn_prompt(qb, kb, vb, cum, nb, t_len)
    y_p, conv_p = _ffn_prompt(xp, o_r, o_f, mod_p, jnp.zeros((nb, CONV_W - 1, 2 * D_FF), F32), fp, nb, t_len)

    heads_t = lambda a: jnp.transpose(a.reshape(nb, N_HEADS, HEAD_DIM, t_len), (0, 3, 1, 2))[None]
    y_prompt = y_p.reshape(nb, t_len, D_MODEL)
    k_prompt = heads_t(kt_p)
    v_prompt = heads_t(vt_p)
    logf_prompt = jnp.transpose(lft, (0, 2, 1))[None]
    wkv_prompt = jnp.swapaxes(st, -1, -2).reshape(1, nb, N_HEADS, HEAD_DIM, HEAD_DIM)
    shift_prompt = zr.reshape(nb, t_len, ZR_PAD)[:, -1, :D_RWKV_IN][None]
    conv_prompt = conv_p[None]

    xs = x_sample.reshape(db, D_MODEL)
    mod_s = _mod(c_sample, w_ada_bf, b_ada2)
    zrt_s, qb_s, k_s, v_s, kt_s, vt_s, lft_s = _inproj_sample(xs, mod_s, g_attn, w_in_bf, bfp)
    shift_t = jnp.pad(state_shift[0].T, ((0, ZR_PAD - D_RWKV_IN), (0, 0)))
    vecs = _rwkv_prep(zrt_s, shift_t, rp)
    hd3 = lambda a: a.reshape(N_HEADS, HEAD_DIM, db)
    o_r_t, wkv_t = _rwkv_step([hd3(a) for a in vecs], rp, jnp.transpose(state_wkv[0], (1, 2, 3, 0)))
    kc = jnp.transpose(cache_k[0], (0, 2, 3, 1))
    vc = jnp.transpose(cache_v[0], (0, 2, 3, 1))
    lfc = jnp.transpose(cache_logf[0], (0, 2, 1))
    to3 = lambda a: a.reshape(db, N_HEADS, HEAD_DIM)
    o_f_s = _attn_sample(to3(qb_s), to3(k_s), to3(v_s), lft_s.T.reshape(db, 1, N_HEADS), kc, vc, lfc,
                         page_table.T)
    cp = state_ffn_conv[0]
    y_s, u_s = _ffn_sample(xs, o_r_t.reshape(D_GROUP, db), o_f_s.reshape(db, D_GROUP), mod_s,
                           cp[:, 0, :], cp[:, 1, :], fp)

    heads_s = lambda a: jnp.transpose(hd3(a), (2, 0, 1)).reshape(1, db, 1, N_HEADS, HEAD_DIM)
    y_sample = y_s.reshape(db, 1, D_MODEL)
    k_sample = heads_s(kt_s)
    v_sample = heads_s(vt_s)
    logf_sample = lft_s.T.reshape(1, db, 1, N_HEADS)
    wkv_sample = jnp.transpose(wkv_t, (3, 0, 1, 2))[None]
    shift_sample = zrt_s[:D_RWKV_IN].T[None]
    conv_sample = jnp.stack([cp[:, 1, :], u_s], axis=1)[None]
    return (y_prompt, y_sample, k_prompt, v_prompt, logf_prompt, wkv_prompt, shift_prompt, conv_prompt,
            k_sample, v_sample, logf_sample, wkv_sample, shift_sample, conv_sample)
```
